```python
import jax, jax.numpy as jnp
from jax import lax
import numpy as np

D_MODEL = 2048
BATCH = 2
SEQ = 8192
DEPTH = 2

GRID_W = 64
HEAD_DIM = 128
N_Q_HEADS = 8
N_KV_HEADS = 2
ATTN_W = N_Q_HEADS * HEAD_DIM
KV_W = N_KV_HEADS * HEAD_DIM
CONV_W = D_MODEL // 4
CONV_K = 3
SGU_GROUPS = 4
SGU_GC = 128
SGU_W = SGU_GROUPS * SGU_GC
CHUNK = 128
Q_BLOCK = 128
MIX_W = ATTN_W + CONV_W + SGU_W
ROPE_THETA = 10000.0
AXIS_ROT = HEAD_DIM // 2
EPS = 1e-6
IN_SPLITS = (ATTN_W, KV_W, KV_W, ATTN_W,
             CONV_W, CONV_W, CONV_W, CONV_W,
             SGU_W, SGU_W, SGU_W)
IN_W = 2 * ATTN_W + 2 * KV_W + 4 * CONV_W + 3 * SGU_W

kernel_name = "hybrid_parallel_attn_conv_sgu_encoder"


def rms_norm(x, w):
    xf = x.astype(jnp.float32)
    y = xf * lax.rsqrt(jnp.mean(xf * xf, axis=-1, keepdims=True) + EPS)
    return (y * w.astype(jnp.float32)).astype(x.dtype)


def axial_rope_tables(seq_len):
    rows = seq_len // GRID_W
    row = jnp.repeat(jnp.arange(rows, dtype=jnp.float32), GRID_W)
    col = jnp.tile(jnp.arange(GRID_W, dtype=jnp.float32), rows)
    inv_freq = ROPE_THETA ** (-jnp.arange(0, AXIS_ROT, 2, dtype=jnp.float32) / AXIS_ROT)
    ang = jnp.stack([row, col], axis=-1)[:, :, None] * inv_freq
    ang = jnp.broadcast_to(ang[:, :, None, :], (seq_len, 2, 2, AXIS_ROT // 2)).reshape(seq_len, HEAD_DIM)
    return jnp.cos(ang), jnp.sin(ang)


def rotate_half_axial(x):
    xa = x.reshape(x.shape[:-1] + (2, 2, AXIS_ROT // 2))
    rot = jnp.stack([-xa[..., 1, :], xa[..., 0, :]], axis=-2)
    return rot.reshape(x.shape)


def attention_branch(q, k, v, q_norm_w, k_norm_w, cos, sin):
    B, S, _ = q.shape
    g = N_Q_HEADS // N_KV_HEADS
    q = rms_norm(q.reshape(B, S, N_Q_HEADS, HEAD_DIM), q_norm_w).astype(jnp.float32)
    k = rms_norm(k.reshape(B, S, N_KV_HEADS, HEAD_DIM), k_norm_w).astype(jnp.float32)
    v = v.reshape(B, S, N_KV_HEADS, HEAD_DIM)
    c, s_ = cos[None, :, None, :], sin[None, :, None, :]
    q = q * c + rotate_half_axial(q) * s_
    k = k * c + rotate_half_axial(k) * s_
    n_blk = S // Q_BLOCK
    qb = q.reshape(B, n_blk, Q_BLOCK, N_KV_HEADS, g, HEAD_DIM).transpose(1, 0, 2, 3, 4, 5)
    scale = HEAD_DIM ** -0.5

    def attend(q_blk):
        scores = jnp.einsum('bqkgd,bskd->bkgqs', q_blk, k) * scale
        p = jax.nn.softmax(scores, axis=-1)
        return jnp.einsum('bkgqs,bskd->bqkgd', p.astype(v.dtype), v)

    o = lax.map(attend, qb)
    return o.transpose(1, 0, 2, 3, 4, 5).reshape(B, S, ATTN_W)


def short_conv_branch(c_in, c_b, c_c, conv_w):
    h = c_c * c_in
    hp = jnp.pad(h, ((0, 0), (1, 1), (0, 0)))
    y = hp[:, :-2] * conv_w[:, 0] + hp[:, 1:-1] * conv_w[:, 1] + hp[:, 2:] * conv_w[:, 2]
    return c_b * y


def sgu_branch(u, v, sgu_norm_w, sgu_w, sgu_b):
    B, S, _ = u.shape
    u = jax.nn.gelu(u, approximate=False)
    v = jax.nn.gelu(v, approximate=False)
    v = rms_norm(v.reshape(B, S, SGU_GROUPS, SGU_GC), sgu_norm_w.reshape(SGU_GROUPS, SGU_GC))
    vc = v.reshape(B, S // CHUNK, CHUNK, SGU_GROUPS, SGU_GC)
    s = jnp.einsum('gpq,bnqgc->bnpgc', sgu_w, vc) + sgu_b.T[:, :, None]
    return u * s.reshape(B, S, SGU_W)


def hybrid_layer(x, cos, sin, norm_w, w_in, q_norm_w, k_norm_w, conv_w, sgu_norm_w, sgu_w, sgu_b,
                 branch_norm_w, w_out):
    h = rms_norm(x, norm_w)
    proj = jnp.einsum('bsd,de->bse', h, w_in)
    cuts, acc = [], 0
    for width in IN_SPLITS[:-1]:
        acc += width
        cuts.append(acc)
    (q, k, v, g_attn, c_in, c_b, c_c, g_conv, s_u, s_v, g_sgu) = jnp.split(proj, cuts, axis=-1)

    o_attn = attention_branch(q, k, v, q_norm_w, k_norm_w, cos, sin)
    o_conv = short_conv_branch(c_in, c_b, c_c, conv_w)
    o_sgu = sgu_branch(s_u, s_v, sgu_norm_w, sgu_w, sgu_b)

    o_attn = rms_norm(o_attn, branch_norm_w[:ATTN_W]) * jax.nn.silu(g_attn)
    o_conv = rms_norm(o_conv, branch_norm_w[ATTN_W:ATTN_W + CONV_W]) * jax.nn.silu(g_conv)
    o_sgu = rms_norm(o_sgu, branch_norm_w[ATTN_W + CONV_W:]) * jax.nn.silu(g_sgu)
    mixed = jnp.concatenate([o_attn, o_conv, o_sgu], axis=-1)
    return x + jnp.einsum('bse,ed->bsd', mixed, w_out)


def setup_inputs(seed: int = 0) -> dict:
    key = jax.random.key(seed)
    ks = jax.random.split(key, 12)

    def nrm(k, shape, scale):
        return jax.random.normal(k, shape, jnp.float32) * scale

    return {
        "x": nrm(ks[0], (BATCH, SEQ, D_MODEL), 1.0),
        "norm_w": 1.0 + nrm(ks[1], (DEPTH, D_MODEL), 0.02),
        "w_in": nrm(ks[2], (DEPTH, D_MODEL, IN_W), D_MODEL ** -0.5),
        "q_norm_w": 1.0 + nrm(ks[3], (DEPTH, HEAD_DIM), 0.02),
        "k_norm_w": 1.0 + nrm(ks[4], (DEPTH, HEAD_DIM), 0.02),
        "conv_w": nrm(ks[5], (DEPTH, CONV_W, CONV_K), CONV_K ** -0.5),
        "sgu_norm_w": 1.0 + nrm(ks[6], (DEPTH, SGU_W), 0.02),
        "sgu_w": nrm(ks[7], (DEPTH, SGU_GROUPS, CHUNK, CHUNK), CHUNK ** -0.5),
        "sgu_b": 1.0 + nrm(ks[8], (DEPTH, SGU_GROUPS, CHUNK), 0.01),
        "branch_norm_w": 1.0 + nrm(ks[9], (DEPTH, MIX_W), 0.02),
        "w_out": nrm(ks[10], (DEPTH, MIX_W, D_MODEL), MIX_W ** -0.5),
        "final_norm_w": 1.0 + nrm(ks[11], (D_MODEL,), 0.02),
    }


def reference(x, norm_w, w_in, q_norm_w, k_norm_w, conv_w, sgu_norm_w, sgu_w, sgu_b,
              branch_norm_w, w_out, final_norm_w):
    cos, sin = axial_rope_tables(x.shape[1])
    for layer in range(DEPTH):
        x = hybrid_layer(x, cos, sin, norm_w[layer], w_in[layer], q_norm_w[layer], k_norm_w[layer],
                         conv_w[layer], sgu_norm_w[layer], sgu_w[layer], sgu_b[layer],
                         branch_norm_w[layer], w_out[layer])
    return rms_norm(x, final_norm_w)
```

```python
import functools

import jax
import jax.numpy as jnp
import numpy as np
from jax import lax
from jax.experimental import pallas as pl
from jax.experimental.pallas import tpu as pltpu

D_MODEL = 2048
HEAD_DIM = 128
N_Q_HEADS = 8
N_KV_HEADS = 2
Q_PER_KV = N_Q_HEADS // N_KV_HEADS
ATTN_W = N_Q_HEADS * HEAD_DIM
KV_W = N_KV_HEADS * HEAD_DIM
CONV_W = 512
SGU_GROUPS = 4
SGU_GC = 128
SGU_W = SGU_GROUPS * SGU_GC
CHUNK = 128
GRID_W = 64
ROPE_THETA = 10000.0
AXIS_ROT = HEAD_DIM // 2
EPS = 1e-6
MIX_W = ATTN_W + CONV_W + SGU_W
ATTN_PROJ_W = 2 * ATTN_W + 2 * KV_W
REST_PROJ_W = 4 * CONV_W + 3 * SGU_W
QK_SCALE = HEAD_DIM ** -0.5

LANES = 128
SUBLANES = 8
VMEM_LIMIT_BYTES = 56 * 1024 * 1024

PROJ_TM = 512
MIX_TM = 512
ATTN_TQ = 512
ATTN_TK = 1024

_BF16 = jnp.bfloat16
_F32 = jnp.float32


def _rms(t):
    return t * lax.rsqrt(jnp.mean(t * t, axis=-1, keepdims=True) + EPS)


def _silu(t):
    return t * (1.0 / (1.0 + jnp.exp(-t)))


def _gelu(t):
    return 0.5 * t * (1.0 + lax.erf(t * np.float32(np.sqrt(0.5))))


def _rope_tables(seq_len):
    rows = seq_len // GRID_W
    row = jnp.repeat(jnp.arange(rows, dtype=_F32), GRID_W)
    col = jnp.tile(jnp.arange(GRID_W, dtype=_F32), rows)
    inv_freq = ROPE_THETA ** (-jnp.arange(0, AXIS_ROT, 2, dtype=_F32) / AXIS_ROT)
    ang = jnp.stack([row, col], axis=-1)[:, :, None] * inv_freq
    ang = jnp.broadcast_to(ang[:, :, None, :], (seq_len, 2, 2, AXIS_ROT // 2)).reshape(seq_len, HEAD_DIM)
    cos, sin = jnp.cos(ang), jnp.sin(ang)
    first_half = (jnp.arange(HEAD_DIM) % AXIS_ROT) < (AXIS_ROT // 2)
    sin_lo = jnp.where(first_half, -sin, 0.0)
    sin_hi = jnp.where(first_half, 0.0, sin)
    return cos, sin_lo, sin_hi


def _attn_proj_kernel(x_ref, nw_ref, w_ref, qw_ref, kw_ref, cos_ref, slo_ref, shi_ref,
                      q_ref, k_ref, v_ref, g_ref, h_scr):
    h_scr[...] = (_rms(x_ref[...]) * nw_ref[...]).astype(_BF16)
    cos, s_lo, s_hi = cos_ref[...], slo_ref[...], shi_ref[...]

    def norm_rope(t, w):
        tn = _rms(t) * w
        return (tn * cos + pltpu.roll(tn, HEAD_DIM - AXIS_ROT // 2, 1) * s_lo
                + pltpu.roll(tn, AXIS_ROT // 2, 1) * s_hi)

    qw, kw = qw_ref[...], kw_ref[...]
    pair = 2 * HEAD_DIM
    for c in range(ATTN_W // pair):
        acc = jnp.dot(h_scr[...], w_ref[:, c * pair:(c + 1) * pair], preferred_element_type=_F32)
        for hh in range(2):
            lo = c * pair + hh * HEAD_DIM
            t = norm_rope(acc[:, hh * HEAD_DIM:(hh + 1) * HEAD_DIM], qw) * QK_SCALE
            q_ref[:, lo:lo + HEAD_DIM] = t.astype(_BF16)
    acc = jnp.dot(h_scr[...], w_ref[:, ATTN_W:ATTN_W + KV_W], preferred_element_type=_F32)
    for hh in range(N_KV_HEADS):
        t = norm_rope(acc[:, hh * HEAD_DIM:(hh + 1) * HEAD_DIM], kw)
        k_ref[:, hh * HEAD_DIM:(hh + 1) * HEAD_DIM] = t.astype(_BF16)
    acc = jnp.dot(h_scr[...], w_ref[:, ATTN_W + KV_W:ATTN_W + 2 * KV_W], preferred_element_type=_F32)
    v_ref[...] = acc.astype(_BF16)
    g0 = ATTN_W + 2 * KV_W
    for c in range(ATTN_W // 512):
        acc = jnp.dot(h_scr[...], w_ref[:, g0 + c * 512:g0 + (c + 1) * 512], preferred_element_type=_F32)
        g_ref[:, c * 512:(c + 1) * 512] = _silu(acc)


def _attn_proj(xf, nw, w_a, qw, kw, cos, s_lo, s_hi, seq_len):
    n = xf.shape[0]
    tm = PROJ_TM
    tiles_per_seq = seq_len // tm
    row = lambda i: (i, 0)
    const = lambda i: (0, 0)
    pos = lambda i: (i % tiles_per_seq, 0)
    return pl.pallas_call(
        _attn_proj_kernel,
        grid=(n // tm,),
        in_specs=[
            pl.BlockSpec((tm, D_MODEL), row),
            pl.BlockSpec((1, D_MODEL), const),
            pl.BlockSpec((D_MODEL, ATTN_PROJ_W), const, pipeline_mode=pl.Buffered(1)),
            pl.BlockSpec((1, HEAD_DIM), const),
            pl.BlockSpec((1, HEAD_DIM), const),
            pl.BlockSpec((tm, HEAD_DIM), pos),
            pl.BlockSpec((tm, HEAD_DIM), pos),
            pl.BlockSpec((tm, HEAD_DIM), pos),
        ],
        out_specs=[
            pl.BlockSpec((tm, ATTN_W), row),
            pl.BlockSpec((tm, KV_W), row),
            pl.BlockSpec((tm, KV_W), row),
            pl.BlockSpec((tm, ATTN_W), row),
        ],
        out_shape=[
            jax.ShapeDtypeStruct((n, ATTN_W), _BF16),
            jax.ShapeDtypeStruct((n, KV_W), _BF16),
            jax.ShapeDtypeStruct((n, KV_W), _BF16),
            jax.ShapeDtypeStruct((n, ATTN_W), _F32),
        ],
        scratch_shapes=[pltpu.VMEM((tm, D_MODEL), _BF16)],
        compiler_params=pltpu.CompilerParams(
            dimension_semantics=("parallel",), vmem_limit_bytes=VMEM_LIMIT_BYTES),
        name="attn_proj",
    )(xf, nw, w_a, qw, kw, cos, s_lo, s_hi)


def _rest_proj_kernel(x_ref, nw_ref, w_ref, snw_ref, sw_ref, sb_ref,
                      hc_ref, cb_ref, gc_ref, os_ref, gs_ref, h_scr):
    tm = x_ref.shape[0]
    h_scr[...] = (_rms(x_ref[...]) * nw_ref[...]).astype(_BF16)

    def proj(idx):
        return jnp.dot(h_scr[...], w_ref[:, idx * 512:(idx + 1) * 512], preferred_element_type=_F32)

    hc_ref[...] = proj(2) * proj(0)
    cb_ref[...] = proj(1)
    gc_ref[...] = _silu(proj(3))
    gs_ref[...] = _silu(proj(6))
    u = _gelu(proj(4))
    v = _gelu(proj(5))
    n_chunks = tm // CHUNK
    for g in range(SGU_GROUPS):
        lo = g * SGU_GC
        vn = (_rms(v[:, lo:lo + SGU_GC]) * snw_ref[:, lo:lo + SGU_GC]).astype(_BF16)
        rhs = jnp.concatenate([vn[c * CHUNK:(c + 1) * CHUNK, :] for c in range(n_chunks)], axis=1)
        s = jnp.dot(sw_ref[g], rhs, preferred_element_type=_F32) + sb_ref[:, g:g + 1]
        for c in range(n_chunks):
            os_ref[c * CHUNK:(c + 1) * CHUNK, lo:lo + SGU_GC] = (
                u[c * CHUNK:(c + 1) * CHUNK, lo:lo + SGU_GC] * s[:, c * SGU_GC:(c + 1) * SGU_GC])


def _rest_proj(xf, nw, w_b, snw, sw, sb_t):
    n = xf.shape[0]
    tm = PROJ_TM
    row = lambda i: (i, 0)
    const = lambda i: (0, 0)
    out = jax.ShapeDtypeStruct((n, 512), _F32)
    return pl.pallas_call(
        _rest_proj_kernel,
        grid=(n // tm,),
        in_specs=[
            pl.BlockSpec((tm, D_MODEL), row),
            pl.BlockSpec((1, D_MODEL), const),
            pl.BlockSpec((D_MODEL, REST_PROJ_W), const, pipeline_mode=pl.Buffered(1)),
            pl.BlockSpec((1, SGU_W), const),
            pl.BlockSpec((SGU_GROUPS, CHUNK, CHUNK), lambda i: (0, 0, 0)),
            pl.BlockSpec((CHUNK, SGU_GROUPS), const),
        ],
        out_specs=[pl.BlockSpec((tm, 512), row)] * 5,
        out_shape=[out] * 5,
        scratch_shapes=[pltpu.VMEM((tm, D_MODEL), _BF16)],
        compiler_params=pltpu.CompilerParams(
            dimension_semantics=("parallel",), vmem_limit_bytes=VMEM_LIMIT_BYTES),
        name="rest_proj",
    )(xf, nw, w_b, snw, sw, sb_t)


def _flash_kernel(q_ref, k_ref, v_ref, o_ref, m_scr, l_scr, acc_scr):
    kv = pl.program_id(3)
    tk = k_ref.shape[1]

    @pl.when(kv == 0)
    def _():
        m_scr[...] = jnp.full(m_scr.shape, -jnp.inf, _F32)
        l_scr[...] = jnp.zeros(l_scr.shape, _F32)
        acc_scr[...] = jnp.zeros(acc_scr.shape, _F32)

    k = k_ref[0]
    v = v_ref[0]
    for g in range(Q_PER_KV):
        q = q_ref[0, :, g * HEAD_DIM:(g + 1) * HEAD_DIM]
        s = lax.dot_general(q, k, (((1,), (1,)), ((), ())), preferred_element_type=_F32)
        m_prev = m_scr[g]
        m_next = jnp.maximum(m_prev, jnp.max(s, axis=1, keepdims=True))
        alpha = jnp.exp(m_prev - m_next)
        p = jnp.exp(s - jnp.tile(m_next, (1, tk // LANES)))
        l_scr[g] = alpha * l_scr[g] + jnp.sum(p, axis=1, keepdims=True)
        acc_scr[g] = alpha * acc_scr[g] + jnp.dot(p.astype(_BF16), v, preferred_element_type=_F32)
        m_scr[g] = m_next

    @pl.when(kv == pl.num_programs(3) - 1)
    def _():
        for g in range(Q_PER_KV):
            o_ref[0, :, g * HEAD_DIM:(g + 1) * HEAD_DIM] = acc_scr[g] / l_scr[g]


def _flash(q, k, v):
    b, s, _ = q.shape
    tq, tk = ATTN_TQ, ATTN_TK
    group_w = Q_PER_KV * HEAD_DIM
    return pl.pallas_call(
        _flash_kernel,
        grid=(b, N_KV_HEADS, s // tq, s // tk),
        in_specs=[
            pl.BlockSpec((1, tq, group_w), lambda bi, hi, qi, ki: (bi, qi, hi)),
            pl.BlockSpec((1, tk, HEAD_DIM), lambda bi, hi, qi, ki: (bi, ki, hi)),
            pl.BlockSpec((1, tk, HEAD_DIM), lambda bi, hi, qi, ki: (bi, ki, hi)),
        ],
        out_specs=pl.BlockSpec((1, tq, group_w), lambda bi, hi, qi, ki: (bi, qi, hi)),
        out_shape=jax.ShapeDtypeStruct((b, s, ATTN_W), _F32),
        scratch_shapes=[
            pltpu.VMEM((Q_PER_KV, tq, LANES), _F32),
            pltpu.VMEM((Q_PER_KV, tq, LANES), _F32),
            pltpu.VMEM((Q_PER_KV, tq, HEAD_DIM), _F32),
        ],
        compiler_params=pltpu.CompilerParams(
            dimension_semantics=("parallel", "parallel", "parallel", "arbitrary"),
            vmem_limit_bytes=VMEM_LIMIT_BYTES),
        name="flash_attn",
    )(q, k, v)


def _mix_kernel(x_ref, oa_ref, ga_ref, hc_ref, hp_ref, hn_ref, cb_ref, gc_ref, os_ref, gs_ref,
                cw_ref, bw_ref, w_ref, fw_ref, out_ref, mixed_scr, *, tiles_per_seq, final_norm):
    i = pl.program_id(0)
    tm = x_ref.shape[0]
    mixed_scr[:, :ATTN_W] = (_rms(oa_ref[...]) * bw_ref[:, :ATTN_W] * ga_ref[...]).astype(_BF16)

    hc = hc_ref[...]
    t = i % tiles_per_seq
    prev_row = jnp.where(t == 0, 0.0, hp_ref[SUBLANES - 1:SUBLANES, :])
    next_row = jnp.where(t == tiles_per_seq - 1, 0.0, hn_ref[0:1, :])
    ridx = lax.broadcasted_iota(jnp.int32, hc.shape, 0)
    up = jnp.where(ridx == 0, prev_row, pltpu.roll(hc, 1, 0))
    down = jnp.where(ridx == tm - 1, next_row, pltpu.roll(hc, tm - 1, 0))
    y = cb_ref[...] * (up * cw_ref[0:1, :] + hc * cw_ref[1:2, :] + down * cw_ref[2:3, :])
    c0 = ATTN_W
    mixed_scr[:, c0:c0 + CONV_W] = (_rms(y) * bw_ref[:, c0:c0 + CONV_W] * gc_ref[...]).astype(_BF16)
    c1 = ATTN_W + CONV_W
    mixed_scr[:, c1:] = (_rms(os_ref[...]) * bw_ref[:, c1:] * gs_ref[...]).astype(_BF16)

    out = x_ref[...] + jnp.dot(mixed_scr[...], w_ref[...], preferred_element_type=_F32)
    if final_norm:
        out = _rms(out) * fw_ref[...]
    out_ref[...] = out


def _mix(xf, oa, ga, hc, cb, gc, osg, gs, cw_t, bw, w_out, fw, seq_len, final_norm):
    n = xf.shape[0]
    tm = MIX_TM
    tiles_per_seq = seq_len // tm
    halo_blocks = tm // SUBLANES
    n_halo = n // SUBLANES
    row = lambda i: (i, 0)
    const = lambda i: (0, 0)
    prev = lambda i: (jnp.maximum(i * halo_blocks - 1, 0), 0)
    nxt = lambda i: (jnp.minimum((i + 1) * halo_blocks, n_halo - 1), 0)
    kern = functools.partial(_mix_kernel, tiles_per_seq=tiles_per_seq, final_norm=final_norm)
    return pl.pallas_call(
        kern,
        grid=(n // tm,),
        in_specs=[
            pl.BlockSpec((tm, D_MODEL), row),
            pl.BlockSpec((tm, ATTN_W), row),
            pl.BlockSpec((tm, ATTN_W), row),
            pl.BlockSpec((tm, CONV_W), row),
            pl.BlockSpec((SUBLANES, CONV_W), prev),
            pl.BlockSpec((SUBLANES, CONV_W), nxt),
            pl.BlockSpec((tm, CONV_W), row),
            pl.BlockSpec((tm, CONV_W), row),
            pl.BlockSpec((tm, SGU_W), row),
            pl.BlockSpec((tm, SGU_W), row),
            pl.BlockSpec((3, CONV_W), const),
            pl.BlockSpec((1, MIX_W), const),
            pl.BlockSpec((MIX_W, D_MODEL), const, pipeline_mode=pl.Buffered(1)),
            pl.BlockSpec((1, D_MODEL), const),
        ],
        out_specs=pl.BlockSpec((tm, D_MODEL), row),
        out_shape=jax.ShapeDtypeStruct((n, D_MODEL), _F32),
        scratch_shapes=[pltpu.VMEM((tm, MIX_W), _BF16)],
        compiler_params=pltpu.CompilerParams(
            dimension_semantics=("parallel",), vmem_limit_bytes=VMEM_LIMIT_BYTES),
        name="mix_out",
    )(xf, oa, ga, hc, hc, hc, cb, gc, osg, gs, cw_t, bw, w_out, fw)


def kernel(x, norm_w, w_in, q_norm_w, k_norm_w, conv_w, sgu_norm_w, sgu_w, sgu_b, branch_norm_w, w_out,
           final_norm_w):
    b, s, d = x.shape
    depth = w_in.shape[0]
    n = b * s
    cos, s_lo, s_hi = _rope_tables(s)
    w_a = w_in[:, :, :ATTN_PROJ_W].astype(_BF16)
    w_b = w_in[:, :, ATTN_PROJ_W:].astype(_BF16)
    w_o = w_out.astype(_BF16)
    sw = sgu_w.astype(_BF16)
    xf = x.reshape(n, d)
    for l in range(depth):
        nw = norm_w[l].reshape(1, d)
        q, k, v, ga = _attn_proj(xf, nw, w_a[l], q_norm_w[l].reshape(1, HEAD_DIM),
                                 k_norm_w[l].reshape(1, HEAD_DIM), cos, s_lo, s_hi, s)
        hc, cb, gc, osg, gs = _rest_proj(xf, nw, w_b[l], sgu_norm_w[l].reshape(1, SGU_W), sw[l],
                                         sgu_b[l].T)
        oa = _flash(q.reshape(b, s, ATTN_W), k.reshape(b, s, KV_W), v.reshape(b, s, KV_W))
        xf = _mix(xf, oa.reshape(n, ATTN_W), ga, hc, cb, gc, osg, gs, conv_w[l].T,
                  branch_norm_w[l].reshape(1, MIX_W), w_o[l], final_norm_w.reshape(1, d), s,
                  final_norm=(l == depth - 1))
    return xf.reshape(b, s, d)
```

```python
import functools

import jax
import jax.numpy as jnp
import numpy as np
from jax import lax
from jax.experimental import pallas as pl
from jax.experimental.pallas import tpu as pltpu

D_MODEL = 2048
HEAD_DIM = 128
N_Q_HEADS = 8
N_KV_HEADS = 2
Q_PER_KV = N_Q_HEADS // N_KV_HEADS
ATTN_W = N_Q_HEADS * HEAD_DIM
KV_W = N_KV_HEADS * HEAD_DIM
CONV_W = 512
SGU_GROUPS = 4
SGU_GC = 128
SGU_W = SGU_GROUPS * SGU_GC
CHUNK = 128
GRID_W = 64
ROPE_THETA = 10000.0
AXIS_ROT = HEAD_DIM // 2
EPS = 1e-6
MIX_W = ATTN_W + CONV_W + SGU_W
ATTN_PROJ_W = 2 * ATTN_W + 2 * KV_W
REST_PROJ_W = 4 * CONV_W + 3 * SGU_W
QK_SCALE_LOG2 = float(HEAD_DIM ** -0.5 * np.log2(np.e))

LANES = 128
SUBLANES = 8
VMEM_LIMIT_BYTES = 56 * 1024 * 1024

PROJ_TM = 512
MIX_TM = 512
ATTN_TQ = 512
ATTN_TK = 512
SOFTMAX_ROWS = 64

_BF16 = jnp.bfloat16
_F32 = jnp.float32


def _rms(t):
    return t * lax.rsqrt(jnp.mean(t * t, axis=-1, keepdims=True) + EPS)


def _silu(t):
    return t * (1.0 / (1.0 + jnp.exp(-t)))


def _gelu(t):
    return 0.5 * t * (1.0 + lax.erf(t * np.float32(np.sqrt(0.5))))


def _rope_tables(seq_len):
    rows = seq_len // GRID_W
    row = jnp.repeat(jnp.arange(rows, dtype=_F32), GRID_W)
    col = jnp.tile(jnp.arange(GRID_W, dtype=_F32), rows)
    inv_freq = ROPE_THETA ** (-jnp.arange(0, AXIS_ROT, 2, dtype=_F32) / AXIS_ROT)
    ang = jnp.stack([row, col], axis=-1)[:, :, None] * inv_freq
    ang = jnp.broadcast_to(ang[:, :, None, :], (seq_len, 2, 2, AXIS_ROT // 2)).reshape(seq_len, HEAD_DIM)
    cos, sin = jnp.cos(ang), jnp.sin(ang)
    first_half = (jnp.arange(HEAD_DIM) % AXIS_ROT) < (AXIS_ROT // 2)
    sin_lo = jnp.where(first_half, -sin, 0.0)
    sin_hi = jnp.where(first_half, 0.0, sin)
    return cos, sin_lo, sin_hi


def _attn_proj_kernel(x_ref, nw_ref, w_ref, qw_ref, kw_ref, cos_ref, slo_ref, shi_ref,
                      q_ref, k_ref, v_ref, g_ref, h_ref):
    h_ref[...] = (_rms(x_ref[...]) * nw_ref[...]).astype(_BF16)
    cos, s_lo, s_hi = cos_ref[...], slo_ref[...], shi_ref[...]

    def norm_rope(t, w):
        tn = _rms(t) * w
        return (tn * cos + pltpu.roll(tn, HEAD_DIM - AXIS_ROT // 2, 1) * s_lo
                + pltpu.roll(tn, AXIS_ROT // 2, 1) * s_hi)

    qw, kw = qw_ref[...], kw_ref[...]
    pair = 2 * HEAD_DIM
    for c in range(ATTN_W // pair):
        acc = jnp.dot(h_ref[...], w_ref[:, c * pair:(c + 1) * pair], preferred_element_type=_F32)
        for hh in range(2):
            t = norm_rope(acc[:, hh * HEAD_DIM:(hh + 1) * HEAD_DIM], qw) * QK_SCALE_LOG2
            q_ref[2 * c + hh] = t.astype(_BF16)
    acc = jnp.dot(h_ref[...], w_ref[:, ATTN_W:ATTN_W + KV_W], preferred_element_type=_F32)
    for hh in range(N_KV_HEADS):
        t = norm_rope(acc[:, hh * HEAD_DIM:(hh + 1) * HEAD_DIM], kw)
        k_ref[:, hh * HEAD_DIM:(hh + 1) * HEAD_DIM] = t.astype(_BF16)
    acc = jnp.dot(h_ref[...], w_ref[:, ATTN_W + KV_W:ATTN_W + 2 * KV_W], preferred_element_type=_F32)
    v_ref[...] = acc.astype(_BF16)
    g0 = ATTN_W + 2 * KV_W
    for c in range(ATTN_W // 512):
        acc = jnp.dot(h_ref[...], w_ref[:, g0 + c * 512:g0 + (c + 1) * 512], preferred_element_type=_F32)
        g_ref[:, c * 512:(c + 1) * 512] = _silu(acc)


def _attn_proj(xf, nw, w_a, qw, kw, cos, s_lo, s_hi, seq_len):
    n = xf.shape[0]
    tm = PROJ_TM
    tiles_per_seq = seq_len // tm
    row = lambda i: (i, 0)
    const = lambda i: (0, 0)
    pos = lambda i: (i % tiles_per_seq, 0)
    return pl.pallas_call(
        _attn_proj_kernel,
        grid=(n // tm,),
        in_specs=[
            pl.BlockSpec((tm, D_MODEL), row),
            pl.BlockSpec((1, D_MODEL), const),
            pl.BlockSpec((D_MODEL, ATTN_PROJ_W), const, pipeline_mode=pl.Buffered(1)),
            pl.BlockSpec((1, HEAD_DIM), const),
            pl.BlockSpec((1, HEAD_DIM), const),
            pl.BlockSpec((tm, HEAD_DIM), pos),
            pl.BlockSpec((tm, HEAD_DIM), pos),
            pl.BlockSpec((tm, HEAD_DIM), pos),
        ],
        out_specs=[
            pl.BlockSpec((N_Q_HEADS, tm, HEAD_DIM), lambda i: (0, i, 0)),
            pl.BlockSpec((tm, KV_W), row),
            pl.BlockSpec((tm, KV_W), row),
            pl.BlockSpec((tm, ATTN_W), row),
            pl.BlockSpec((tm, D_MODEL), row),
        ],
        out_shape=[
            jax.ShapeDtypeStruct((N_Q_HEADS, n, HEAD_DIM), _BF16),
            jax.ShapeDtypeStruct((n, KV_W), _BF16),
            jax.ShapeDtypeStruct((n, KV_W), _BF16),
            jax.ShapeDtypeStruct((n, ATTN_W), _F32),
            jax.ShapeDtypeStruct((n, D_MODEL), _BF16),
        ],
        compiler_params=pltpu.CompilerParams(
            dimension_semantics=("parallel",), vmem_limit_bytes=VMEM_LIMIT_BYTES),
        name="attn_proj",
    )(xf, nw, w_a, qw, kw, cos, s_lo, s_hi)


def _rest_proj_kernel(h_ref, w_ref, snw_ref, sw_ref, sb_ref, hc_ref, cb_ref, gc_ref, os_ref, gs_ref):
    tm = h_ref.shape[0]

    def proj(idx):
        return jnp.dot(h_ref[...], w_ref[:, idx * 512:(idx + 1) * 512], preferred_element_type=_F32)

    hc_ref[...] = proj(2) * proj(0)
    cb_ref[...] = proj(1)
    gc_ref[...] = _silu(proj(3))
    gs_ref[...] = _silu(proj(6))
    u = _gelu(proj(4))
    v = _gelu(proj(5))
    n_chunks = tm // CHUNK
    for g in range(SGU_GROUPS):
        lo = g * SGU_GC
        vn = (_rms(v[:, lo:lo + SGU_GC]) * snw_ref[:, lo:lo + SGU_GC]).astype(_BF16)
        rhs = jnp.concatenate([vn[c * CHUNK:(c + 1) * CHUNK, :] for c in range(n_chunks)], axis=1)
        s = jnp.dot(sw_ref[g], rhs, preferred_element_type=_F32) + sb_ref[:, g:g + 1]
        for c in range(n_chunks):
            os_ref[c * CHUNK:(c + 1) * CHUNK, lo:lo + SGU_GC] = (
                u[c * CHUNK:(c + 1) * CHUNK, lo:lo + SGU_GC] * s[:, c * SGU_GC:(c + 1) * SGU_GC])


def _rest_proj(h, w_b, snw, sw, sb_t):
    n = h.shape[0]
    tm = PROJ_TM
    row = lambda i: (i, 0)
    const = lambda i: (0, 0)
    out = jax.ShapeDtypeStruct((n, 512), _F32)
    return pl.pallas_call(
        _rest_proj_kernel,
        grid=(n // tm,),
        in_specs=[
            pl.BlockSpec((tm, D_MODEL), row),
            pl.BlockSpec((D_MODEL, REST_PROJ_W), const, pipeline_mode=pl.Buffered(1)),
            pl.BlockSpec((1, SGU_W), const),
            pl.BlockSpec((SGU_GROUPS, CHUNK, CHUNK), lambda i: (0, 0, 0)),
            pl.BlockSpec((CHUNK, SGU_GROUPS), const),
        ],
        out_specs=[pl.BlockSpec((tm, 512), row)] * 5,
        out_shape=[out] * 5,
        compiler_params=pltpu.CompilerParams(
            dimension_semantics=("parallel",), vmem_limit_bytes=VMEM_LIMIT_BYTES),
        name="rest_proj",
    )(h, w_b, snw, sw, sb_t)


def _flash_kernel(q_ref, k_ref, v_ref, o_ref, s_scr, p_scr, m_scr, l_scr, acc_scr, *, tk, rb):
    tq = q_ref.shape[1]
    n_chunks = k_ref.shape[0] // tk

    m_scr[...] = jnp.full(m_scr.shape, -jnp.inf, _F32)
    l_scr[...] = jnp.zeros(l_scr.shape, _F32)
    acc_scr[...] = jnp.zeros(acc_scr.shape, _F32)
    p_scr[1] = jnp.zeros(p_scr.shape[1:], _BF16)

    def kv_rows(c):
        return pl.ds(pl.multiple_of(c * tk, tk), tk)

    def qk(g, c):
        return lax.dot_general(q_ref[g], k_ref[kv_rows(c), :], (((1,), (1,)), ((), ())),
                               preferred_element_type=_F32)

    def pv(g, slot, c):
        acc_scr[g] += jnp.dot(p_scr[slot], v_ref[kv_rows(c), :], preferred_element_type=_F32)

    def softmax(g, slot):
        for r in range(tq // rb):
            rows = slice(r * rb, (r + 1) * rb)
            s = s_scr[slot, rows, :]
            m_prev = m_scr[g, rows, :]
            m_next = jnp.maximum(m_prev, jnp.max(s, axis=1, keepdims=True))
            alpha = jnp.exp2(m_prev - m_next)
            p = jnp.exp2(s - jnp.tile(m_next, (1, tk // LANES)))
            l_scr[g, rows, :] = alpha * l_scr[g, rows, :] + jnp.sum(p, axis=1, keepdims=True)
            m_scr[g, rows, :] = m_next
            acc_scr[g, rows, :] = alpha * acc_scr[g, rows, :]
            p_scr[slot, rows, :] = p.astype(_BF16)

    s_scr[0] = qk(0, 0)

    def body(c, carry):
        for g in range(Q_PER_KV):
            slot = g % 2
            if g < Q_PER_KV - 1:
                s_scr[1 - slot] = qk(g + 1, c)
            else:
                s_scr[1 - slot] = qk(0, jnp.minimum(c + 1, n_chunks - 1))
            softmax(g, slot)
            if g == 0:
                pv(Q_PER_KV - 1, 1, jnp.maximum(c - 1, 0))
            else:
                pv(g - 1, 1 - slot, c)
        return carry

    lax.fori_loop(0, n_chunks, body, 0)
    pv(Q_PER_KV - 1, 1, n_chunks - 1)

    for g in range(Q_PER_KV):
        o_ref[:, g * HEAD_DIM:(g + 1) * HEAD_DIM] = acc_scr[g] / l_scr[g]


def _flash(q, k, v, batch, seq_len):
    tq, tk = ATTN_TQ, ATTN_TK
    assert Q_PER_KV % 2 == 0
    q_tiles = seq_len // tq
    group_w = Q_PER_KV * HEAD_DIM
    kern = functools.partial(_flash_kernel, tk=tk, rb=SOFTMAX_ROWS)
    return pl.pallas_call(
        kern,
        grid=(batch, N_KV_HEADS, q_tiles),
        in_specs=[
            pl.BlockSpec((Q_PER_KV, tq, HEAD_DIM), lambda bi, hi, qi: (hi, bi * q_tiles + qi, 0)),
            pl.BlockSpec((seq_len, HEAD_DIM), lambda bi, hi, qi: (bi, hi)),
            pl.BlockSpec((seq_len, HEAD_DIM), lambda bi, hi, qi: (bi, hi)),
        ],
        out_specs=pl.BlockSpec((tq, group_w), lambda bi, hi, qi: (bi * q_tiles + qi, hi)),
        out_shape=jax.ShapeDtypeStruct((batch * seq_len, ATTN_W), _F32),
        scratch_shapes=[
            pltpu.VMEM((2, tq, tk), _F32),
            pltpu.VMEM((2, tq, tk), _BF16),
            pltpu.VMEM((Q_PER_KV, tq, LANES), _F32),
            pltpu.VMEM((Q_PER_KV, tq, LANES), _F32),
            pltpu.VMEM((Q_PER_KV, tq, HEAD_DIM), _F32),
        ],
        compiler_params=pltpu.CompilerParams(
            dimension_semantics=("parallel", "parallel", "parallel"),
            vmem_limit_bytes=VMEM_LIMIT_BYTES),
        name="flash_attn",
    )(q, k, v)


def _mix_kernel(x_ref, oa_ref, ga_ref, hc_ref, hp_ref, hn_ref, cb_ref, gc_ref, os_ref, gs_ref,
                cw_ref, bw_ref, w_ref, fw_ref, out_ref, mixed_scr, *, tiles_per_seq, final_norm):
    i = pl.program_id(0)
    tm = x_ref.shape[0]
    mixed_scr[:, :ATTN_W] = (_rms(oa_ref[...]) * bw_ref[:, :ATTN_W] * ga_ref[...]).astype(_BF16)

    hc = hc_ref[...]
    t = i % tiles_per_seq
    prev_row = jnp.where(t == 0, 0.0, hp_ref[SUBLANES - 1:SUBLANES, :])
    next_row = jnp.where(t == tiles_per_seq - 1, 0.0, hn_ref[0:1, :])
    ridx = lax.broadcasted_iota(jnp.int32, hc.shape, 0)
    up = jnp.where(ridx == 0, prev_row, pltpu.roll(hc, 1, 0))
    down = jnp.where(ridx == tm - 1, next_row, pltpu.roll(hc, tm - 1, 0))
    y = cb_ref[...] * (up * cw_ref[0:1, :] + hc * cw_ref[1:2, :] + down * cw_ref[2:3, :])
    c0 = ATTN_W
    mixed_scr[:, c0:c0 + CONV_W] = (_rms(y) * bw_ref[:, c0:c0 + CONV_W] * gc_ref[...]).astype(_BF16)
    c1 = ATTN_W + CONV_W
    mixed_scr[:, c1:] = (_rms(os_ref[...]) * bw_ref[:, c1:] * gs_ref[...]).astype(_BF16)

    out = x_ref[...] + jnp.dot(mixed_scr[...], w_ref[...], preferred_element_type=_F32)
    if final_norm:
        out = _rms(out) * fw_ref[...]
    out_ref[...] = out


def _mix(xf, oa, ga, hc, cb, gc, osg, gs, cw_t, bw, w_out, fw, seq_len, final_norm):
    n = xf.shape[0]
    tm = MIX_TM
    tiles_per_seq = seq_len // tm
    halo_blocks = tm // SUBLANES
    n_halo = n // SUBLANES
    row = lambda i: (i, 0)
    const = lambda i: (0, 0)
    prev = lambda i: (jnp.maximum(i * halo_blocks - 1, 0), 0)
    nxt = lambda i: (jnp.minimum((i + 1) * halo_blocks, n_halo - 1), 0)
    kern = functools.partial(_mix_kernel, tiles_per_seq=tiles_per_seq, final_norm=final_norm)
    return pl.pallas_call(
        kern,
        grid=(n // tm,),
        in_specs=[
            pl.BlockSpec((tm, D_MODEL), row),
            pl.BlockSpec((tm, ATTN_W), row),
            pl.BlockSpec((tm, ATTN_W), row),
            pl.BlockSpec((tm, CONV_W), row),
            pl.BlockSpec((SUBLANES, CONV_W), prev),
            pl.BlockSpec((SUBLANES, CONV_W), nxt),
            pl.BlockSpec((tm, CONV_W), row),
            pl.BlockSpec((tm, CONV_W), row),
            pl.BlockSpec((tm, SGU_W), row),
            pl.BlockSpec((tm, SGU_W), row),
            pl.BlockSpec((3, CONV_W), const),
            pl.BlockSpec((1, MIX_W), const),
            pl.BlockSpec((MIX_W, D_MODEL), const, pipeline_mode=pl.Buffered(1)),
            pl.BlockSpec((1, D_MODEL), const),
        ],
        out_specs=pl.BlockSpec((tm, D_MODEL), row),
        out_shape=jax.ShapeDtypeStruct((n, D_MODEL), _F32),
        scratch_shapes=[pltpu.VMEM((tm, MIX_W), _BF16)],
        compiler_params=pltpu.CompilerParams(
            dimension_semantics=("parallel",), vmem_limit_bytes=VMEM_LIMIT_BYTES),
        name="mix_out",
    )(xf, oa, ga, hc, hc, hc, cb, gc, osg, gs, cw_t, bw, w_out, fw)


def kernel(x, norm_w, w_in, q_norm_w, k_norm_w, conv_w, sgu_norm_w, sgu_w, sgu_b, branch_norm_w, w_out,
           final_norm_w):
    b, s, d = x.shape
    depth = w_in.shape[0]
    n = b * s
    cos, s_lo, s_hi = _rope_tables(s)
    w_a = w_in[:, :, :ATTN_PROJ_W].astype(_BF16)
    w_b = w_in[:, :, ATTN_PROJ_W:].astype(_BF16)
    w_o = w_out.astype(_BF16)
    sw = sgu_w.astype(_BF16)
    xf = x.reshape(n, d)
    for l in range(depth):
        q, k, v, ga, h = _attn_proj(xf, norm_w[l].reshape(1, d), w_a[l], q_norm_w[l].reshape(1, HEAD_DIM),
                                    k_norm_w[l].reshape(1, HEAD_DIM), cos, s_lo, s_hi, s)
        hc, cb, gc, osg, gs = _rest_proj(h, w_b[l], sgu_norm_w[l].reshape(1, SGU_W), sw[l], sgu_b[l].T)
        oa = _flash(q, k, v, b, s)
        xf = _mix(xf, oa, ga, hc, cb, gc, osg, gs, conv_w[l].T, branch_norm_w[l].reshape(1, MIX_W),
                  w_o[l], final_norm_w.reshape(1, d), s, final_norm=(l == depth - 1))
    return xf.reshape(b, s, d)
```

```python
import functools

import jax
import jax.numpy as jnp
import numpy as np
from jax import lax
from jax.experimental import pallas as pl
from jax.experimental.pallas import tpu as pltpu

D_MODEL = 2048
HEAD_DIM = 128
N_Q_HEADS = 8
N_KV_HEADS = 2
Q_PER_KV = N_Q_HEADS // N_KV_HEADS
ATTN_W = N_Q_HEADS * HEAD_DIM
KV_W = N_KV_HEADS * HEAD_DIM
CONV_W = 512
SGU_GROUPS = 4
SGU_GC = 128
SGU_W = SGU_GROUPS * SGU_GC
CHUNK = 128
GRID_W = 64
ROPE_THETA = 10000.0
AXIS_ROT = HEAD_DIM // 2
EPS = 1e-6
MIX_W = ATTN_W + CONV_W + SGU_W
ATTN_PROJ_W = 2 * ATTN_W + 2 * KV_W
REST_PROJ_W = 4 * CONV_W + 3 * SGU_W
QK_SCALE_LOG2 = float(HEAD_DIM ** -0.5 * np.log2(np.e))

LANES = 128
SUBLANES = 8
VMEM_LIMIT_BYTES = 56 * 1024 * 1024

PROJ_TM = 512
MIX_TM = 512
ATTN_TQ = 512
BOUNDED_TK = 1024
ONLINE_TK = 512
SOFTMAX_ROWS = 64
assert Q_PER_KV % 2 == 0

MAX_BOUNDED_SHIFT = 48.0

_BF16 = jnp.bfloat16
_F32 = jnp.float32


def _rms(t):
    return t * lax.rsqrt(jnp.mean(t * t, axis=-1, keepdims=True) + EPS)


def _silu(t):
    return t * (1.0 / (1.0 + jnp.exp(-t)))


def _gelu(t):
    return 0.5 * t * (1.0 + lax.erf(t * np.float32(np.sqrt(0.5))))


def _rope_tables(seq_len):
    rows = seq_len // GRID_W
    row = jnp.repeat(jnp.arange(rows, dtype=_F32), GRID_W)
    col = jnp.tile(jnp.arange(GRID_W, dtype=_F32), rows)
    inv_freq = ROPE_THETA ** (-jnp.arange(0, AXIS_ROT, 2, dtype=_F32) / AXIS_ROT)
    ang = jnp.stack([row, col], axis=-1)[:, :, None] * inv_freq
    ang = jnp.broadcast_to(ang[:, :, None, :], (seq_len, 2, 2, AXIS_ROT // 2)).reshape(seq_len, HEAD_DIM)
    cos, sin = jnp.cos(ang), jnp.sin(ang)
    first_half = (jnp.arange(HEAD_DIM) % AXIS_ROT) < (AXIS_ROT // 2)
    sin_lo = jnp.where(first_half, -sin, 0.0)
    sin_hi = jnp.where(first_half, 0.0, sin)
    return cos, sin_lo, sin_hi


def _attn_proj_kernel(x_ref, nw_ref, w_ref, qw_ref, kw_ref, cos_ref, slo_ref, shi_ref,
                      q_ref, k_ref, v_ref, g_ref, h_ref):
    tm = x_ref.shape[0]
    h_ref[...] = (_rms(x_ref[...]) * nw_ref[...]).astype(_BF16)
    cos, s_lo, s_hi = cos_ref[...], slo_ref[...], shi_ref[...]

    def norm_rope(t, w):
        tn = _rms(t) * w
        return (tn * cos + pltpu.roll(tn, HEAD_DIM - AXIS_ROT // 2, 1) * s_lo
                + pltpu.roll(tn, AXIS_ROT // 2, 1) * s_hi)

    qw, kw = qw_ref[...], kw_ref[...]
    pair = 2 * HEAD_DIM
    for c in range(ATTN_W // pair):
        acc = jnp.dot(h_ref[...], w_ref[:, c * pair:(c + 1) * pair], preferred_element_type=_F32)
        for hh in range(2):
            t = norm_rope(acc[:, hh * HEAD_DIM:(hh + 1) * HEAD_DIM], qw) * QK_SCALE_LOG2
            q_ref[2 * c + hh] = t.astype(_BF16)
    acc = jnp.dot(h_ref[...], w_ref[:, ATTN_W:ATTN_W + KV_W], preferred_element_type=_F32)
    for hh in range(N_KV_HEADS):
        t = norm_rope(acc[:, hh * HEAD_DIM:(hh + 1) * HEAD_DIM], kw)
        k_ref[:, hh * HEAD_DIM:(hh + 1) * HEAD_DIM] = t.astype(_BF16)
    acc = jnp.dot(h_ref[...], w_ref[:, ATTN_W + KV_W:ATTN_W + 2 * KV_W], preferred_element_type=_F32)
    ones = jnp.ones((tm, HEAD_DIM), _BF16)
    for hh in range(N_KV_HEADS):
        v_ref[:, 2 * hh * HEAD_DIM:(2 * hh + 1) * HEAD_DIM] = acc[:, hh * HEAD_DIM:(hh + 1) * HEAD_DIM].astype(_BF16)
        v_ref[:, (2 * hh + 1) * HEAD_DIM:(2 * hh + 2) * HEAD_DIM] = ones
    g0 = ATTN_W + 2 * KV_W
    for c in range(ATTN_W // 512):
        acc = jnp.dot(h_ref[...], w_ref[:, g0 + c * 512:g0 + (c + 1) * 512], preferred_element_type=_F32)
        g_ref[:, c * 512:(c + 1) * 512] = _silu(acc)


def _attn_proj(xf, nw, w_a, qw, kw, cos, s_lo, s_hi, seq_len):
    n = xf.shape[0]
    tm = PROJ_TM
    tiles_per_seq = seq_len // tm
    row = lambda i: (i, 0)
    const = lambda i: (0, 0)
    pos = lambda i: (i % tiles_per_seq, 0)
    return pl.pallas_call(
        _attn_proj_kernel,
        grid=(n // tm,),
        in_specs=[
            pl.BlockSpec((tm, D_MODEL), row),
            pl.BlockSpec((1, D_MODEL), const),
            pl.BlockSpec((D_MODEL, ATTN_PROJ_W), const, pipeline_mode=pl.Buffered(1)),
            pl.BlockSpec((1, HEAD_DIM), const),
            pl.BlockSpec((1, HEAD_DIM), const),
            pl.BlockSpec((tm, HEAD_DIM), pos),
            pl.BlockSpec((tm, HEAD_DIM), pos),
            pl.BlockSpec((tm, HEAD_DIM), pos),
        ],
        out_specs=[
            pl.BlockSpec((N_Q_HEADS, tm, HEAD_DIM), lambda i: (0, i, 0)),
            pl.BlockSpec((tm, KV_W), row),
            pl.BlockSpec((tm, 2 * KV_W), row),
            pl.BlockSpec((tm, ATTN_W), row),
            pl.BlockSpec((tm, D_MODEL), row),
        ],
        out_shape=[
            jax.ShapeDtypeStruct((N_Q_HEADS, n, HEAD_DIM), _BF16),
            jax.ShapeDtypeStruct((n, KV_W), _BF16),
            jax.ShapeDtypeStruct((n, 2 * KV_W), _BF16),
            jax.ShapeDtypeStruct((n, ATTN_W), _F32),
            jax.ShapeDtypeStruct((n, D_MODEL), _BF16),
        ],
        compiler_params=pltpu.CompilerParams(
            dimension_semantics=("parallel",), vmem_limit_bytes=VMEM_LIMIT_BYTES),
        name="attn_proj",
    )(xf, nw, w_a, qw, kw, cos, s_lo, s_hi)


def _rest_proj_kernel(h_ref, w_ref, snw_ref, sw_ref, sb_ref, hc_ref, cb_ref, gc_ref, os_ref, gs_ref):
    tm = h_ref.shape[0]

    def proj(idx):
        return jnp.dot(h_ref[...], w_ref[:, idx * 512:(idx + 1) * 512], preferred_element_type=_F32)

    hc_ref[...] = proj(2) * proj(0)
    cb_ref[...] = proj(1)
    gc_ref[...] = _silu(proj(3))
    gs_ref[...] = _silu(proj(6))
    u = _gelu(proj(4))
    v = _gelu(proj(5))
    n_chunks = tm // CHUNK
    for g in range(SGU_GROUPS):
        lo = g * SGU_GC
        vn = (_rms(v[:, lo:lo + SGU_GC]) * snw_ref[:, lo:lo + SGU_GC]).astype(_BF16)
        rhs = jnp.concatenate([vn[c * CHUNK:(c + 1) * CHUNK, :] for c in range(n_chunks)], axis=1)
        s = jnp.dot(sw_ref[g], rhs, preferred_element_type=_F32) + sb_ref[:, g:g + 1]
        for c in range(n_chunks):
            os_ref[c * CHUNK:(c + 1) * CHUNK, lo:lo + SGU_GC] = (
                u[c * CHUNK:(c + 1) * CHUNK, lo:lo + SGU_GC] * s[:, c * SGU_GC:(c + 1) * SGU_GC])


def _rest_proj(h, w_b, snw, sw, sb_t):
    n = h.shape[0]
    tm = PROJ_TM
    row = lambda i: (i, 0)
    const = lambda i: (0, 0)
    out = jax.ShapeDtypeStruct((n, 512), _F32)
    return pl.pallas_call(
        _rest_proj_kernel,
        grid=(n // tm,),
        in_specs=[
            pl.BlockSpec((tm, D_MODEL), row),
            pl.BlockSpec((D_MODEL, REST_PROJ_W), const, pipeline_mode=pl.Buffered(1)),
            pl.BlockSpec((1, SGU_W), const),
            pl.BlockSpec((SGU_GROUPS, CHUNK, CHUNK), lambda i: (0, 0, 0)),
            pl.BlockSpec((CHUNK, SGU_GROUPS), const),
        ],
        out_specs=[pl.BlockSpec((tm, 512), row)] * 5,
        out_shape=[out] * 5,
        compiler_params=pltpu.CompilerParams(
            dimension_semantics=("parallel",), vmem_limit_bytes=VMEM_LIMIT_BYTES),
        name="rest_proj",
    )(h, w_b, snw, sw, sb_t)


def _flash_online_kernel(q_ref, k_ref, v_ref, o_ref, s_scr, p_scr, m_scr, l_scr, acc_scr, *, tk, rb):
    tq = q_ref.shape[1]
    n_chunks = k_ref.shape[0] // tk

    m_scr[...] = jnp.full(m_scr.shape, -jnp.inf, _F32)
    l_scr[...] = jnp.zeros(l_scr.shape, _F32)
    acc_scr[...] = jnp.zeros(acc_scr.shape, _F32)
    p_scr[1] = jnp.zeros(p_scr.shape[1:], _BF16)

    def kv_rows(c):
        return pl.ds(pl.multiple_of(c * tk, tk), tk)

    def qk(g, c):
        return lax.dot_general(q_ref[g], k_ref[kv_rows(c), :], (((1,), (1,)), ((), ())),
                               preferred_element_type=_F32)

    def pv(g, slot, c):
        acc_scr[g] += jnp.dot(p_scr[slot], v_ref[kv_rows(c), :], preferred_element_type=_F32)

    def softmax(g, slot):
        for r in range(tq // rb):
            rows = slice(r * rb, (r + 1) * rb)
            s = s_scr[slot, rows, :]
            m_prev = m_scr[g, rows, :]
            m_next = jnp.maximum(m_prev, jnp.max(s, axis=1, keepdims=True))
            alpha = jnp.exp2(m_prev - m_next)
            p = jnp.exp2(s - jnp.tile(m_next, (1, tk // LANES)))
            l_scr[g, rows, :] = alpha * l_scr[g, rows, :] + jnp.sum(p, axis=1, keepdims=True)
            m_scr[g, rows, :] = m_next
            acc_scr[g, rows, :] = alpha * acc_scr[g, rows, :]
            p_scr[slot, rows, :] = p.astype(_BF16)

    s_scr[0] = qk(0, 0)

    def body(c, carry):
        for g in range(Q_PER_KV):
            slot = g % 2
            if g < Q_PER_KV - 1:
                s_scr[1 - slot] = qk(g + 1, c)
            else:
                s_scr[1 - slot] = qk(0, jnp.minimum(c + 1, n_chunks - 1))
            softmax(g, slot)
            if g == 0:
                pv(Q_PER_KV - 1, 1, jnp.maximum(c - 1, 0))
            else:
                pv(g - 1, 1 - slot, c)
        return carry

    lax.fori_loop(0, n_chunks, body, 0)
    pv(Q_PER_KV - 1, 1, n_chunks - 1)

    for g in range(Q_PER_KV):
        o_ref[:, g * HEAD_DIM:(g + 1) * HEAD_DIM] = acc_scr[g] / l_scr[g]


def _flash_online(q, k, v1, batch, seq_len):
    tq, tk = ATTN_TQ, ONLINE_TK
    q_tiles = seq_len // tq
    group_w = Q_PER_KV * HEAD_DIM
    kern = functools.partial(_flash_online_kernel, tk=tk, rb=SOFTMAX_ROWS)
    return pl.pallas_call(
        kern,
        grid=(batch, N_KV_HEADS, q_tiles),
        in_specs=[
            pl.BlockSpec((Q_PER_KV, tq, HEAD_DIM), lambda bi, hi, qi: (hi, bi * q_tiles + qi, 0)),
            pl.BlockSpec((seq_len, HEAD_DIM), lambda bi, hi, qi: (bi, hi)),
            pl.BlockSpec((seq_len, HEAD_DIM), lambda bi, hi, qi: (bi, 2 * hi)),
        ],
        out_specs=pl.BlockSpec((tq, group_w), lambda bi, hi, qi: (bi * q_tiles + qi, hi)),
        out_shape=jax.ShapeDtypeStruct((batch * seq_len, ATTN_W), _F32),
        scratch_shapes=[
            pltpu.VMEM((2, tq, tk), _F32),
            pltpu.VMEM((2, tq, tk), _BF16),
            pltpu.VMEM((Q_PER_KV, tq, LANES), _F32),
            pltpu.VMEM((Q_PER_KV, tq, LANES), _F32),
            pltpu.VMEM((Q_PER_KV, tq, HEAD_DIM), _F32),
        ],
        compiler_params=pltpu.CompilerParams(
            dimension_semantics=("parallel", "parallel", "parallel"),
            vmem_limit_bytes=VMEM_LIMIT_BYTES),
        name="flash_online",
    )(q, k, v1)


def _flash_bounded_kernel(shift_ref, q_ref, k_ref, v_ref, o_ref, p_scr, acc_scr, *, tk):
    n_chunks = k_ref.shape[0] // tk
    shift = jnp.tile(shift_ref[...], (1, tk // LANES))

    acc_scr[...] = jnp.zeros(acc_scr.shape, _F32)
    p_scr[1] = jnp.zeros(p_scr.shape[1:], _BF16)

    def kv_rows(c):
        return pl.ds(pl.multiple_of(c * tk, tk), tk)

    def probs(g, c, slot):
        s = lax.dot_general(q_ref[g], k_ref[kv_rows(c), :], (((1,), (1,)), ((), ())),
                            preferred_element_type=_F32)
        p_scr[slot] = jnp.exp2(s - shift).astype(_BF16)

    def pv(g, slot, c):
        acc_scr[g] += jnp.dot(p_scr[slot], v_ref[kv_rows(c), :], preferred_element_type=_F32)

    def body(c, carry):
        for g in range(Q_PER_KV):
            slot = g % 2
            probs(g, c, slot)
            if g == 0:
                pv(Q_PER_KV - 1, 1, jnp.maximum(c - 1, 0))
            else:
                pv(g - 1, 1 - slot, c)
        return carry

    lax.fori_loop(0, n_chunks, body, 0)
    pv(Q_PER_KV - 1, 1, n_chunks - 1)

    for g in range(Q_PER_KV):
        a = acc_scr[g]
        o_ref[:, g * HEAD_DIM:(g + 1) * HEAD_DIM] = a[:, :HEAD_DIM] / a[:, HEAD_DIM:]


def _flash_bounded(q, k, v1, shift, batch, seq_len):
    tq, tk = ATTN_TQ, BOUNDED_TK
    q_tiles = seq_len // tq
    group_w = Q_PER_KV * HEAD_DIM
    kern = functools.partial(_flash_bounded_kernel, tk=tk)
    return pl.pallas_call(
        kern,
        grid=(batch, N_KV_HEADS, q_tiles),
        in_specs=[
            pl.BlockSpec((1, LANES), lambda bi, hi, qi: (0, 0)),
            pl.BlockSpec((Q_PER_KV, tq, HEAD_DIM), lambda bi, hi, qi: (hi, bi * q_tiles + qi, 0)),
            pl.BlockSpec((seq_len, HEAD_DIM), lambda bi, hi, qi: (bi, hi)),
            pl.BlockSpec((seq_len, 2 * HEAD_DIM), lambda bi, hi, qi: (bi, hi)),
        ],
        out_specs=pl.BlockSpec((tq, group_w), lambda bi, hi, qi: (bi * q_tiles + qi, hi)),
        out_shape=jax.ShapeDtypeStruct((batch * seq_len, ATTN_W), _F32),
        scratch_shapes=[
            pltpu.VMEM((2, tq, tk), _BF16),
            pltpu.VMEM((Q_PER_KV, tq, 2 * HEAD_DIM), _F32),
        ],
        compiler_params=pltpu.CompilerParams(
            dimension_semantics=("parallel", "parallel", "parallel"),
            vmem_limit_bytes=VMEM_LIMIT_BYTES),
        name="flash_bounded",
    )(shift, q, k, v1)


def _score_bound(q_norm_w, k_norm_w):
    rounding = (1.0 + 2.0 ** -8) ** 2
    return (HEAD_DIM * QK_SCALE_LOG2 * rounding) * jnp.max(jnp.abs(q_norm_w)) * jnp.max(jnp.abs(k_norm_w))


def _attention(q, k, v1, q_norm_w, k_norm_w, batch, seq_len):
    shift = _score_bound(q_norm_w, k_norm_w)
    return lax.cond(
        shift <= MAX_BOUNDED_SHIFT,
        lambda: _flash_bounded(q, k, v1, jnp.full((1, LANES), shift, _F32), batch, seq_len),
        lambda: _flash_online(q, k, v1, batch, seq_len))


def _mix_kernel(x_ref, oa_ref, ga_ref, hc_ref, hp_ref, hn_ref, cb_ref, gc_ref, os_ref, gs_ref,
                cw_ref, bw_ref, w_ref, fw_ref, out_ref, mixed_scr, *, tiles_per_seq, final_norm):
    i = pl.program_id(0)
    tm = x_ref.shape[0]
    mixed_scr[:, :ATTN_W] = (_rms(oa_ref[...]) * bw_ref[:, :ATTN_W] * ga_ref[...]).astype(_BF16)

    hc = hc_ref[...]
    t = i % tiles_per_seq
    prev_row = jnp.where(t == 0, 0.0, hp_ref[SUBLANES - 1:SUBLANES, :])
    next_row = jnp.where(t == tiles_per_seq - 1, 0.0, hn_ref[0:1, :])
    ridx = lax.broadcasted_iota(jnp.int32, hc.shape, 0)
    up = jnp.where(ridx == 0, prev_row, pltpu.roll(hc, 1, 0))
    down = jnp.where(ridx == tm - 1, next_row, pltpu.roll(hc, tm - 1, 0))
    y = cb_ref[...] * (up * cw_ref[0:1, :] + hc * cw_ref[1:2, :] + down * cw_ref[2:3, :])
    c0 = ATTN_W
    mixed_scr[:, c0:c0 + CONV_W] = (_rms(y) * bw_ref[:, c0:c0 + CONV_W] * gc_ref[...]).astype(_BF16)
    c1 = ATTN_W + CONV_W
    mixed_scr[:, c1:] = (_rms(os_ref[...]) * bw_ref[:, c1:] * gs_ref[...]).astype(_BF16)

    out = x_ref[...] + jnp.dot(mixed_scr[...], w_ref[...], preferred_element_type=_F32)
    if final_norm:
        out = _rms(out) * fw_ref[...]
    out_ref[...] = out


def _mix(xf, oa, ga, hc, cb, gc, osg, gs, cw_t, bw, w_out, fw, seq_len, final_norm):
    n = xf.shape[0]
    tm = MIX_TM
    tiles_per_seq = seq_len // tm
    halo_blocks = tm // SUBLANES
    n_halo = n // SUBLANES
    row = lambda i: (i, 0)
    const = lambda i: (0, 0)
    prev = lambda i: (jnp.maximum(i * halo_blocks - 1, 0), 0)
    nxt = lambda i: (jnp.minimum((i + 1) * halo_blocks, n_halo - 1), 0)
    kern = functools.partial(_mix_kernel, tiles_per_seq=tiles_per_seq, final_norm=final_norm)
    return pl.pallas_call(
        kern,
        grid=(n // tm,),
        in_specs=[
            pl.BlockSpec((tm, D_MODEL), row),
            pl.BlockSpec((tm, ATTN_W), row),
            pl.BlockSpec((tm, ATTN_W), row),
            pl.BlockSpec((tm, CONV_W), row),
            pl.BlockSpec((SUBLANES, CONV_W), prev),
            pl.BlockSpec((SUBLANES, CONV_W), nxt),
            pl.BlockSpec((tm, CONV_W), row),
            pl.BlockSpec((tm, CONV_W), row),
            pl.BlockSpec((tm, SGU_W), row),
            pl.BlockSpec((tm, SGU_W), row),
            pl.BlockSpec((3, CONV_W), const),
            pl.BlockSpec((1, MIX_W), const),
            pl.BlockSpec((MIX_W, D_MODEL), const, pipeline_mode=pl.Buffered(1)),
            pl.BlockSpec((1, D_MODEL), const),
        ],
        out_specs=pl.BlockSpec((tm, D_MODEL), row),
        out_shape=jax.ShapeDtypeStruct((n, D_MODEL), _F32),
        scratch_shapes=[pltpu.VMEM((tm, MIX_W), _BF16)],
        compiler_params=pltpu.CompilerParams(
            dimension_semantics=("parallel",), vmem_limit_bytes=VMEM_LIMIT_BYTES),
        name="mix_out",
    )(xf, oa, ga, hc, hc, hc, cb, gc, osg, gs, cw_t, bw, w_out, fw)


def kernel(x, norm_w, w_in, q_norm_w, k_norm_w, conv_w, sgu_norm_w, sgu_w, sgu_b, branch_norm_w, w_out,
           final_norm_w):
    b, s, d = x.shape
    depth = w_in.shape[0]
    n = b * s
    cos, s_lo, s_hi = _rope_tables(s)
    w_a = w_in[:, :, :ATTN_PROJ_W].astype(_BF16)
    w_b = w_in[:, :, ATTN_PROJ_W:].astype(_BF16)
    w_o = w_out.astype(_BF16)
    sw = sgu_w.astype(_BF16)
    xf = x.reshape(n, d)
    for l in range(depth):
        q, k, v1, ga, h = _attn_proj(xf, norm_w[l].reshape(1, d), w_a[l], q_norm_w[l].reshape(1, HEAD_DIM),
                                     k_norm_w[l].reshape(1, HEAD_DIM), cos, s_lo, s_hi, s)
        hc, cb, gc, osg, gs = _rest_proj(h, w_b[l], sgu_norm_w[l].reshape(1, SGU_W), sw[l], sgu_b[l].T)
        oa = _attention(q, k, v1, q_norm_w[l], k_norm_w[l], b, s)
        xf = _mix(xf, oa, ga, hc, cb, gc, osg, gs, conv_w[l].T, branch_norm_w[l].reshape(1, MIX_W),
                  w_o[l], final_norm_w.reshape(1, d), s, final_norm=(l == depth - 1))
    return xf.reshape(b, s, d)
```

```python
import functools

import jax
import jax.numpy as jnp
import numpy as np
from jax import lax
from jax.experimental import pallas as pl
from jax.experimental.pallas import tpu as pltpu

D_MODEL = 2048
HEAD_DIM = 128
N_Q_HEADS = 8
N_KV_HEADS = 2
Q_PER_KV = N_Q_HEADS // N_KV_HEADS
ATTN_W = N_Q_HEADS * HEAD_DIM
KV_W = N_KV_HEADS * HEAD_DIM
CONV_W = 512
SGU_GROUPS = 4
SGU_GC = 128
SGU_W = SGU_GROUPS * SGU_GC
CHUNK = 128
GRID_W = 64
ROPE_THETA = 10000.0
AXIS_ROT = HEAD_DIM // 2
EPS = 1e-6
MIX_W = ATTN_W + CONV_W + SGU_W
ATTN_PROJ_W = 2 * ATTN_W + 2 * KV_W
REST_BLOCK_W = 512
REST_BLOCKS = 7
assert CONV_W == SGU_W == REST_BLOCK_W and ATTN_PROJ_W % REST_BLOCK_W == 0
QK_SCALE_LOG2 = float(HEAD_DIM ** -0.5 * np.log2(np.e))

LANES = 128
SUBLANES = 8
VMEM_LIMIT_BYTES = 56 * 1024 * 1024

PROJ_TM = 512
MIX_TM = 512
ATTN_TQ = 512
BOUNDED_TK = 2048
ONLINE_TK = 512
SOFTMAX_ROWS = 64
assert Q_PER_KV % 2 == 0

MAX_BOUNDED_SHIFT = 48.0

_BF16 = jnp.bfloat16
_F32 = jnp.float32


def _rms(t):
    return t * lax.rsqrt(jnp.mean(t * t, axis=-1, keepdims=True) + EPS)


def _silu(t):
    return t * (1.0 / (1.0 + jnp.exp(-t)))


def _gelu(t):
    return 0.5 * t * (1.0 + lax.erf(t * np.float32(np.sqrt(0.5))))


def _rope_tables(seq_len):
    rows = seq_len // GRID_W
    row = jnp.repeat(jnp.arange(rows, dtype=_F32), GRID_W)
    col = jnp.tile(jnp.arange(GRID_W, dtype=_F32), rows)
    inv_freq = ROPE_THETA ** (-jnp.arange(0, AXIS_ROT, 2, dtype=_F32) / AXIS_ROT)
    ang = jnp.stack([row, col], axis=-1)[:, :, None] * inv_freq
    ang = jnp.broadcast_to(ang[:, :, None, :], (seq_len, 2, 2, AXIS_ROT // 2)).reshape(seq_len, HEAD_DIM)
    cos, sin = jnp.cos(ang), jnp.sin(ang)
    first_half = (jnp.arange(HEAD_DIM) % AXIS_ROT) < (AXIS_ROT // 2)
    sin_lo = jnp.where(first_half, -sin, 0.0)
    sin_hi = jnp.where(first_half, 0.0, sin)
    return cos, sin_lo, sin_hi


def _attn_proj_kernel(x_ref, nw_ref, w_ref, qw_ref, kw_ref, cos_ref, slo_ref, shi_ref,
                      q_ref, k_ref, v_ref, g_ref, h_ref):
    tm = x_ref.shape[0]
    h_ref[...] = (_rms(x_ref[...]) * nw_ref[...]).astype(_BF16)
    cos, s_lo, s_hi = cos_ref[...], slo_ref[...], shi_ref[...]

    def norm_rope(t, w):
        tn = _rms(t) * w
        return (tn * cos + pltpu.roll(tn, HEAD_DIM - AXIS_ROT // 2, 1) * s_lo
                + pltpu.roll(tn, AXIS_ROT // 2, 1) * s_hi)

    qw, kw = qw_ref[...], kw_ref[...]
    pair = 2 * HEAD_DIM
    for c in range(ATTN_W // pair):
        acc = jnp.dot(h_ref[...], w_ref[:, c * pair:(c + 1) * pair], preferred_element_type=_F32)
        for hh in range(2):
            t = norm_rope(acc[:, hh * HEAD_DIM:(hh + 1) * HEAD_DIM], qw) * QK_SCALE_LOG2
            q_ref[2 * c + hh] = t.astype(_BF16)
    acc = jnp.dot(h_ref[...], w_ref[:, ATTN_W:ATTN_W + KV_W], preferred_element_type=_F32)
    for hh in range(N_KV_HEADS):
        t = norm_rope(acc[:, hh * HEAD_DIM:(hh + 1) * HEAD_DIM], kw)
        k_ref[:, hh * HEAD_DIM:(hh + 1) * HEAD_DIM] = t.astype(_BF16)
    acc = jnp.dot(h_ref[...], w_ref[:, ATTN_W + KV_W:ATTN_W + 2 * KV_W], preferred_element_type=_F32)
    ones = jnp.ones((tm, HEAD_DIM), _BF16)
    for hh in range(N_KV_HEADS):
        v_ref[:, 2 * hh * HEAD_DIM:(2 * hh + 1) * HEAD_DIM] = acc[:, hh * HEAD_DIM:(hh + 1) * HEAD_DIM].astype(_BF16)
        v_ref[:, (2 * hh + 1) * HEAD_DIM:(2 * hh + 2) * HEAD_DIM] = ones
    g0 = ATTN_W + 2 * KV_W
    for c in range(ATTN_W // 512):
        acc = jnp.dot(h_ref[...], w_ref[:, g0 + c * 512:g0 + (c + 1) * 512], preferred_element_type=_F32)
        g_ref[:, c * 512:(c + 1) * 512] = _silu(acc)


def _attn_proj(xf, nw, w_in, layer, qw, kw, cos, s_lo, s_hi, seq_len):
    n = xf.shape[0]
    tm = PROJ_TM
    tiles_per_seq = seq_len // tm
    row = lambda i: (i, 0)
    const = lambda i: (0, 0)
    pos = lambda i: (i % tiles_per_seq, 0)
    return pl.pallas_call(
        _attn_proj_kernel,
        grid=(n // tm,),
        in_specs=[
            pl.BlockSpec((tm, D_MODEL), row),
            pl.BlockSpec((1, D_MODEL), const),
            pl.BlockSpec((None, D_MODEL, ATTN_PROJ_W), lambda i: (layer, 0, 0), pipeline_mode=pl.Buffered(1)),
            pl.BlockSpec((1, HEAD_DIM), const),
            pl.BlockSpec((1, HEAD_DIM), const),
            pl.BlockSpec((tm, HEAD_DIM), pos),
            pl.BlockSpec((tm, HEAD_DIM), pos),
            pl.BlockSpec((tm, HEAD_DIM), pos),
        ],
        out_specs=[
            pl.BlockSpec((N_Q_HEADS, tm, HEAD_DIM), lambda i: (0, i, 0)),
            pl.BlockSpec((tm, KV_W), row),
            pl.BlockSpec((tm, 2 * KV_W), row),
            pl.BlockSpec((tm, ATTN_W), row),
            pl.BlockSpec((tm, D_MODEL), row),
        ],
        out_shape=[
            jax.ShapeDtypeStruct((N_Q_HEADS, n, HEAD_DIM), _BF16),
            jax.ShapeDtypeStruct((n, KV_W), _BF16),
            jax.ShapeDtypeStruct((n, 2 * KV_W), _BF16),
            jax.ShapeDtypeStruct((n, ATTN_W), _F32),
            jax.ShapeDtypeStruct((n, D_MODEL), _BF16),
        ],
        compiler_params=pltpu.CompilerParams(
            dimension_semantics=("parallel",), vmem_limit_bytes=VMEM_LIMIT_BYTES),
        name="attn_proj",
    )(xf, nw, w_in, qw, kw, cos, s_lo, s_hi)


def _rest_proj_kernel(h_ref, *refs):
    w_refs = refs[:REST_BLOCKS]
    snw_ref, sw_ref, sb_ref, hc_ref, cb_ref, gc_ref, os_ref, gs_ref = refs[REST_BLOCKS:]
    tm = h_ref.shape[0]

    def proj(idx):
        return jnp.dot(h_ref[...], w_refs[idx][...], preferred_element_type=_F32)

    hc_ref[...] = proj(2) * proj(0)
    cb_ref[...] = proj(1)
    gc_ref[...] = _silu(proj(3))
    gs_ref[...] = _silu(proj(6))
    u = _gelu(proj(4))
    v = _gelu(proj(5))
    n_chunks = tm // CHUNK
    for g in range(SGU_GROUPS):
        lo = g * SGU_GC
        vn = (_rms(v[:, lo:lo + SGU_GC]) * snw_ref[:, lo:lo + SGU_GC]).astype(_BF16)
        rhs = jnp.concatenate([vn[c * CHUNK:(c + 1) * CHUNK, :] for c in range(n_chunks)], axis=1)
        s = jnp.dot(sw_ref[g], rhs, preferred_element_type=_F32) + sb_ref[:, g:g + 1]
        for c in range(n_chunks):
            os_ref[c * CHUNK:(c + 1) * CHUNK, lo:lo + SGU_GC] = (
                u[c * CHUNK:(c + 1) * CHUNK, lo:lo + SGU_GC] * s[:, c * SGU_GC:(c + 1) * SGU_GC])


def _rest_proj(h, w_in, layer, snw, sw, sb_t):
    n = h.shape[0]
    tm = PROJ_TM
    row = lambda i: (i, 0)
    const = lambda i: (0, 0)
    out = jax.ShapeDtypeStruct((n, REST_BLOCK_W), _F32)
    first_block = ATTN_PROJ_W // REST_BLOCK_W
    w_specs = [
        pl.BlockSpec((None, D_MODEL, REST_BLOCK_W), (lambda i, j=j: (layer, 0, first_block + j)),
                     pipeline_mode=pl.Buffered(1))
        for j in range(REST_BLOCKS)
    ]
    return pl.pallas_call(
        _rest_proj_kernel,
        grid=(n // tm,),
        in_specs=[
            pl.BlockSpec((tm, D_MODEL), row),
            *w_specs,
            pl.BlockSpec((1, SGU_W), const),
            pl.BlockSpec((SGU_GROUPS, CHUNK, CHUNK), lambda i: (0, 0, 0)),
            pl.BlockSpec((CHUNK, SGU_GROUPS), const),
        ],
        out_specs=[pl.BlockSpec((tm, REST_BLOCK_W), row)] * 5,
        out_shape=[out] * 5,
        compiler_params=pltpu.CompilerParams(
            dimension_semantics=("parallel",), vmem_limit_bytes=VMEM_LIMIT_BYTES),
        name="rest_proj",
    )(h, *([w_in] * REST_BLOCKS), snw, sw, sb_t)


def _flash_online_kernel(q_ref, k_ref, v_ref, o_ref, s_scr, p_scr, m_scr, l_scr, acc_scr, *, tk, rb):
    tq = q_ref.shape[1]
    n_chunks = k_ref.shape[0] // tk

    m_scr[...] = jnp.full(m_scr.shape, -jnp.inf, _F32)
    l_scr[...] = jnp.zeros(l_scr.shape, _F32)
    acc_scr[...] = jnp.zeros(acc_scr.shape, _F32)
    p_scr[1] = jnp.zeros(p_scr.shape[1:], _BF16)

    def kv_rows(c):
        return pl.ds(pl.multiple_of(c * tk, tk), tk)

    def qk(g, c):
        return lax.dot_general(q_ref[g], k_ref[kv_rows(c), :], (((1,), (1,)), ((), ())),
                               preferred_element_type=_F32)

    def pv(g, slot, c):
        acc_scr[g] += jnp.dot(p_scr[slot], v_ref[kv_rows(c), :], preferred_element_type=_F32)

    def softmax(g, slot):
        for r in range(tq // rb):
            rows = slice(r * rb, (r + 1) * rb)
            s = s_scr[slot, rows, :]
            m_prev = m_scr[g, rows, :]
            m_next = jnp.maximum(m_prev, jnp.max(s, axis=1, keepdims=True))
            alpha = jnp.exp2(m_prev - m_next)
            p = jnp.exp2(s - jnp.tile(m_next, (1, tk // LANES)))
            l_scr[g, rows, :] = alpha * l_scr[g, rows, :] + jnp.sum(p, axis=1, keepdims=True)
            m_scr[g, rows, :] = m_next
            acc_scr[g, rows, :] = alpha * acc_scr[g, rows, :]
            p_scr[slot, rows, :] = p.astype(_BF16)

    s_scr[0] = qk(0, 0)

    def body(c, carry):
        for g in range(Q_PER_KV):
            slot = g % 2
            if g < Q_PER_KV - 1:
                s_scr[1 - slot] = qk(g + 1, c)
            else:
                s_scr[1 - slot] = qk(0, jnp.minimum(c + 1, n_chunks - 1))
            softmax(g, slot)
            if g == 0:
                pv(Q_PER_KV - 1, 1, jnp.maximum(c - 1, 0))
            else:
                pv(g - 1, 1 - slot, c)
        return carry

    lax.fori_loop(0, n_chunks, body, 0)
    pv(Q_PER_KV - 1, 1, n_chunks - 1)

    for g in range(Q_PER_KV):
        o_ref[:, g * HEAD_DIM:(g + 1) * HEAD_DIM] = acc_scr[g] / l_scr[g]


def _flash_online(q, k, v1, batch, seq_len):
    tq, tk = ATTN_TQ, ONLINE_TK
    q_tiles = seq_len // tq
    group_w = Q_PER_KV * HEAD_DIM
    kern = functools.partial(_flash_online_kernel, tk=tk, rb=SOFTMAX_ROWS)
    return pl.pallas_call(
        kern,
        grid=(batch, N_KV_HEADS, q_tiles),
        in_specs=[
            pl.BlockSpec((Q_PER_KV, tq, HEAD_DIM), lambda bi, hi, qi: (hi, bi * q_tiles + qi, 0)),
            pl.BlockSpec((seq_len, HEAD_DIM), lambda bi, hi, qi: (bi, hi)),
            pl.BlockSpec((seq_len, HEAD_DIM), lambda bi, hi, qi: (bi, 2 * hi)),
        ],
        out_specs=pl.BlockSpec((tq, group_w), lambda bi, hi, qi: (bi * q_tiles + qi, hi)),
        out_shape=jax.ShapeDtypeStruct((batch * seq_len, ATTN_W), _F32),
        scratch_shapes=[
            pltpu.VMEM((2, tq, tk), _F32),
            pltpu.VMEM((2, tq, tk), _BF16),
            pltpu.VMEM((Q_PER_KV, tq, LANES), _F32),
            pltpu.VMEM((Q_PER_KV, tq, LANES), _F32),
            pltpu.VMEM((Q_PER_KV, tq, HEAD_DIM), _F32),
        ],
        compiler_params=pltpu.CompilerParams(
            dimension_semantics=("parallel", "parallel", "parallel"),
            vmem_limit_bytes=VMEM_LIMIT_BYTES),
        name="flash_online",
    )(q, k, v1)


def _flash_bounded_kernel(shift_ref, q_ref, k_ref, v_ref, o_ref, p_scr, acc_scr, *, tk):
    n_chunks = k_ref.shape[0] // tk
    shift = jnp.tile(shift_ref[...], (1, tk // LANES))

    def kv_rows(c):
        return pl.ds(pl.multiple_of(c * tk, tk), tk)

    def probs(g, c, slot):
        s = lax.dot_general(q_ref[g], k_ref[kv_rows(c), :], (((1,), (1,)), ((), ())),
                            preferred_element_type=_F32)
        p_scr[slot] = jnp.exp2(s - shift).astype(_BF16)

    def pv(g, slot, c, first):
        d = jnp.dot(p_scr[slot], v_ref[kv_rows(c), :], preferred_element_type=_F32)
        if first:
            acc_scr[g] = d
        else:
            acc_scr[g] += d

    def chunk(c, first=False, last=False):
        for g in range(Q_PER_KV):
            slot = g % 2
            if g < Q_PER_KV - 1:
                probs(g + 1, c, 1 - slot)
            elif not last:
                probs(0, c + 1, 1 - slot)
            pv(g, slot, c, first)

    probs(0, 0, 0)
    chunk(0, first=True)

    def body(c, carry):
        chunk(c)
        return carry

    lax.fori_loop(1, n_chunks - 1, body, 0)
    chunk(n_chunks - 1, last=True)

    for g in range(Q_PER_KV):
        a = acc_scr[g]
        o_ref[:, g * HEAD_DIM:(g + 1) * HEAD_DIM] = a[:, :HEAD_DIM] / a[:, HEAD_DIM:]


def _flash_bounded(q, k, v1, shift, batch, seq_len):
    tq, tk = ATTN_TQ, BOUNDED_TK
    assert seq_len // tk >= 2
    q_tiles = seq_len // tq
    group_w = Q_PER_KV * HEAD_DIM
    kern = functools.partial(_flash_bounded_kernel, tk=tk)
    return pl.pallas_call(
        kern,
        grid=(batch, N_KV_HEADS, q_tiles),
        in_specs=[
            pl.BlockSpec((1, LANES), lambda bi, hi, qi: (0, 0)),
            pl.BlockSpec((Q_PER_KV, tq, HEAD_DIM), lambda bi, hi, qi: (hi, bi * q_tiles + qi, 0)),
            pl.BlockSpec((seq_len, HEAD_DIM), lambda bi, hi, qi: (bi, hi)),
            pl.BlockSpec((seq_len, 2 * HEAD_DIM), lambda bi, hi, qi: (bi, hi)),
        ],
        out_specs=pl.BlockSpec((tq, group_w), lambda bi, hi, qi: (bi * q_tiles + qi, hi)),
        out_shape=jax.ShapeDtypeStruct((batch * seq_len, ATTN_W), _F32),
        scratch_shapes=[
            pltpu.VMEM((2, tq, tk), _BF16),
            pltpu.VMEM((Q_PER_KV, tq, 2 * HEAD_DIM), _F32),
        ],
        compiler_params=pltpu.CompilerParams(
            dimension_semantics=("parallel", "parallel", "parallel"),
            vmem_limit_bytes=VMEM_LIMIT_BYTES),
        name="flash_bounded",
    )(shift, q, k, v1)


def _score_bound(q_norm_w, k_norm_w):
    rounding = (1.0 + 2.0 ** -8) ** 2
    return (HEAD_DIM * QK_SCALE_LOG2 * rounding) * jnp.max(jnp.abs(q_norm_w)) * jnp.max(jnp.abs(k_norm_w))


def _attention(q, k, v1, q_norm_w, k_norm_w, batch, seq_len):
    shift = _score_bound(q_norm_w, k_norm_w)
    return lax.cond(
        shift <= MAX_BOUNDED_SHIFT,
        lambda: _flash_bounded(q, k, v1, jnp.full((1, LANES), shift, _F32), batch, seq_len),
        lambda: _flash_online(q, k, v1, batch, seq_len))


def _mix_kernel(x_ref, oa_ref, ga_ref, hc_ref, hp_ref, hn_ref, cb_ref, gc_ref, os_ref, gs_ref,
                cw_ref, bw_ref, w_ref, fw_ref, out_ref, mixed_scr, *, tiles_per_seq, final_norm):
    i = pl.program_id(0)
    tm = x_ref.shape[0]
    mixed_scr[:, :ATTN_W] = (_rms(oa_ref[...]) * bw_ref[:, :ATTN_W] * ga_ref[...]).astype(_BF16)

    hc = hc_ref[...]
    t = i % tiles_per_seq
    prev_row = jnp.where(t == 0, 0.0, hp_ref[SUBLANES - 1:SUBLANES, :])
    next_row = jnp.where(t == tiles_per_seq - 1, 0.0, hn_ref[0:1, :])
    ridx = lax.broadcasted_iota(jnp.int32, hc.shape, 0)
    up = jnp.where(ridx == 0, prev_row, pltpu.roll(hc, 1, 0))
    down = jnp.where(ridx == tm - 1, next_row, pltpu.roll(hc, tm - 1, 0))
    y = cb_ref[...] * (up * cw_ref[0:1, :] + hc * cw_ref[1:2, :] + down * cw_ref[2:3, :])
    c0 = ATTN_W
    mixed_scr[:, c0:c0 + CONV_W] = (_rms(y) * bw_ref[:, c0:c0 + CONV_W] * gc_ref[...]).astype(_BF16)
    c1 = ATTN_W + CONV_W
    mixed_scr[:, c1:] = (_rms(os_ref[...]) * bw_ref[:, c1:] * gs_ref[...]).astype(_BF16)

    out = x_ref[...] + jnp.dot(mixed_scr[...], w_ref[...], preferred_element_type=_F32)
    if final_norm:
        out = _rms(out) * fw_ref[...]
    out_ref[...] = out


def _mix(xf, oa, ga, hc, cb, gc, osg, gs, cw_t, bw, w_out, layer, fw, seq_len, final_norm):
    n = xf.shape[0]
    tm = MIX_TM
    tiles_per_seq = seq_len // tm
    halo_blocks = tm // SUBLANES
    n_halo = n // SUBLANES
    row = lambda i: (i, 0)
    const = lambda i: (0, 0)
    prev = lambda i: (jnp.maximum(i * halo_blocks - 1, 0), 0)
    nxt = lambda i: (jnp.minimum((i + 1) * halo_blocks, n_halo - 1), 0)
    kern = functools.partial(_mix_kernel, tiles_per_seq=tiles_per_seq, final_norm=final_norm)
    return pl.pallas_call(
        kern,
        grid=(n // tm,),
        in_specs=[
            pl.BlockSpec((tm, D_MODEL), row),
            pl.BlockSpec((tm, ATTN_W), row),
            pl.BlockSpec((tm, ATTN_W), row),
            pl.BlockSpec((tm, CONV_W), row),
            pl.BlockSpec((SUBLANES, CONV_W), prev),
            pl.BlockSpec((SUBLANES, CONV_W), nxt),
            pl.BlockSpec((tm, CONV_W), row),
            pl.BlockSpec((tm, CONV_W), row),
            pl.BlockSpec((tm, SGU_W), row),
            pl.BlockSpec((tm, SGU_W), row),
            pl.BlockSpec((3, CONV_W), const),
            pl.BlockSpec((1, MIX_W), const),
            pl.BlockSpec((None, MIX_W, D_MODEL), lambda i: (layer, 0, 0), pipeline_mode=pl.Buffered(1)),
            pl.BlockSpec((1, D_MODEL), const),
        ],
        out_specs=pl.BlockSpec((tm, D_MODEL), row),
        out_shape=jax.ShapeDtypeStruct((n, D_MODEL), _F32),
        scratch_shapes=[pltpu.VMEM((tm, MIX_W), _BF16)],
        compiler_params=pltpu.CompilerParams(
            dimension_semantics=("parallel",), vmem_limit_bytes=VMEM_LIMIT_BYTES),
        name="mix_out",
    )(xf, oa, ga, hc, hc, hc, cb, gc, osg, gs, cw_t, bw, w_out, fw)


def kernel(x, norm_w, w_in, q_norm_w, k_norm_w, conv_w, sgu_norm_w, sgu_w, sgu_b, branch_norm_w, w_out,
           final_norm_w):
    b, s, d = x.shape
    depth = w_in.shape[0]
    n = b * s
    cos, s_lo, s_hi = _rope_tables(s)
    sw = sgu_w.astype(_BF16)
    xf = x.reshape(n, d)
    for l in range(depth):
        q, k, v1, ga, h = _attn_proj(xf, norm_w[l].reshape(1, d), w_in, l, q_norm_w[l].reshape(1, HEAD_DIM),
                                     k_norm_w[l].reshape(1, HEAD_DIM), cos, s_lo, s_hi, s)
        hc, cb, gc, osg, gs = _rest_proj(h, w_in, l, sgu_norm_w[l].reshape(1, SGU_W), sw[l], sgu_b[l].T)
        oa = _attention(q, k, v1, q_norm_w[l], k_norm_w[l], b, s)
        xf = _mix(xf, oa, ga, hc, cb, gc, osg, gs, conv_w[l].T, branch_norm_w[l].reshape(1, MIX_W),
                  w_out, l, final_norm_w.reshape(1, d), s, final_norm=(l == depth - 1))
    return xf.reshape(b, s, d)
```

```python
import functools

import jax
import jax.numpy as jnp
import numpy as np
from jax import lax
from jax.experimental import pallas as pl
from jax.experimental.pallas import tpu as pltpu

D_MODEL = 2048
HEAD_DIM = 128
N_Q_HEADS = 8
N_KV_HEADS = 2
Q_PER_KV = N_Q_HEADS // N_KV_HEADS
ATTN_W = N_Q_HEADS * HEAD_DIM
KV_W = N_KV_HEADS * HEAD_DIM
CONV_W = 512
SGU_GROUPS = 4
SGU_GC = 128
SGU_W = SGU_GROUPS * SGU_GC
CHUNK = 128
GRID_W = 64
ROPE_THETA = 10000.0
AXIS_ROT = HEAD_DIM // 2
EPS = 1e-6
MIX_W = ATTN_W + CONV_W + SGU_W
ATTN_PROJ_W = 2 * ATTN_W + 2 * KV_W
REST_BLOCK_W = 512
REST_BLOCKS = 7
assert CONV_W == SGU_W == REST_BLOCK_W and ATTN_PROJ_W % REST_BLOCK_W == 0
QK_SCALE_LOG2 = float(HEAD_DIM ** -0.5 * np.log2(np.e))

LANES = 128
SUBLANES = 8
VMEM_LIMIT_BYTES = 56 * 1024 * 1024

PROJ_TM = 512
MIX_TM = 512
ATTN_TQ = 512
BOUNDED_TK = 2048
ONLINE_TK = 512
SOFTMAX_ROWS = 64
assert Q_PER_KV % 2 == 0

MAX_BOUNDED_SHIFT = 48.0

_BF16 = jnp.bfloat16
_F32 = jnp.float32


def _rms(t):
    return t * lax.rsqrt(jnp.mean(t * t, axis=-1, keepdims=True) + EPS)


def _silu(t):
    return t * (1.0 / (1.0 + jnp.exp(-t)))


def _gelu(t):
    return 0.5 * t * (1.0 + lax.erf(t * np.float32(np.sqrt(0.5))))


def _rope_tables(seq_len):
    rows = seq_len // GRID_W
    row = jnp.repeat(jnp.arange(rows, dtype=_F32), GRID_W)
    col = jnp.tile(jnp.arange(GRID_W, dtype=_F32), rows)
    inv_freq = ROPE_THETA ** (-jnp.arange(0, AXIS_ROT, 2, dtype=_F32) / AXIS_ROT)
    ang = jnp.stack([row, col], axis=-1)[:, :, None] * inv_freq
    ang = jnp.broadcast_to(ang[:, :, None, :], (seq_len, 2, 2, AXIS_ROT // 2)).reshape(seq_len, HEAD_DIM)
    cos, sin = jnp.cos(ang), jnp.sin(ang)
    first_half = (jnp.arange(HEAD_DIM) % AXIS_ROT) < (AXIS_ROT // 2)
    sin_lo = jnp.where(first_half, -sin, 0.0)
    sin_hi = jnp.where(first_half, 0.0, sin)
    return cos, sin_lo, sin_hi


def _attn_proj_kernel(x_ref, nw_ref, w_ref, qw_ref, kw_ref, cos_ref, slo_ref, shi_ref,
                      q_ref, k_ref, v_ref, g_ref, h_ref):
    tm = x_ref.shape[0]
    h_ref[...] = (_rms(x_ref[...]) * nw_ref[...]).astype(_BF16)
    cos, s_lo, s_hi = cos_ref[...], slo_ref[...], shi_ref[...]

    def norm_rope(t, w):
        tn = _rms(t) * w
        return (tn * cos + pltpu.roll(tn, HEAD_DIM - AXIS_ROT // 2, 1) * s_lo
                + pltpu.roll(tn, AXIS_ROT // 2, 1) * s_hi)

    qw, kw = qw_ref[...], kw_ref[...]
    pair = 2 * HEAD_DIM
    for c in range(ATTN_W // pair):
        acc = jnp.dot(h_ref[...], w_ref[:, c * pair:(c + 1) * pair], preferred_element_type=_F32)
        for hh in range(2):
            t = norm_rope(acc[:, hh * HEAD_DIM:(hh + 1) * HEAD_DIM], qw) * QK_SCALE_LOG2
            q_ref[2 * c + hh] = t.astype(_BF16)
    acc = jnp.dot(h_ref[...], w_ref[:, ATTN_W:ATTN_W + KV_W], preferred_element_type=_F32)
    for hh in range(N_KV_HEADS):
        t = norm_rope(acc[:, hh * HEAD_DIM:(hh + 1) * HEAD_DIM], kw)
        k_ref[:, hh * HEAD_DIM:(hh + 1) * HEAD_DIM] = t.astype(_BF16)
    acc = jnp.dot(h_ref[...], w_ref[:, ATTN_W + KV_W:ATTN_W + 2 * KV_W], preferred_element_type=_F32)
    ones = jnp.ones((tm, HEAD_DIM), _BF16)
    for hh in range(N_KV_HEADS):
        v_ref[:, 2 * hh * HEAD_DIM:(2 * hh + 1) * HEAD_DIM] = acc[:, hh * HEAD_DIM:(hh + 1) * HEAD_DIM].astype(_BF16)
        v_ref[:, (2 * hh + 1) * HEAD_DIM:(2 * hh + 2) * HEAD_DIM] = ones
    g0 = ATTN_W + 2 * KV_W
    for c in range(ATTN_W // 512):
        acc = jnp.dot(h_ref[...], w_ref[:, g0 + c * 512:g0 + (c + 1) * 512], preferred_element_type=_F32)
        g_ref[:, c * 512:(c + 1) * 512] = _silu(acc).astype(_BF16)


def _attn_proj(xf, nw, w_in, layer, qw, kw, cos, s_lo, s_hi, seq_len):
    n = xf.shape[0]
    tm = PROJ_TM
    tiles_per_seq = seq_len // tm
    row = lambda i: (i, 0)
    const = lambda i: (0, 0)
    pos = lambda i: (i % tiles_per_seq, 0)
    return pl.pallas_call(
        _attn_proj_kernel,
        grid=(n // tm,),
        in_specs=[
            pl.BlockSpec((tm, D_MODEL), row),
            pl.BlockSpec((1, D_MODEL), const),
            pl.BlockSpec((None, D_MODEL, ATTN_PROJ_W), lambda i: (layer, 0, 0), pipeline_mode=pl.Buffered(1)),
            pl.BlockSpec((1, HEAD_DIM), const),
            pl.BlockSpec((1, HEAD_DIM), const),
            pl.BlockSpec((tm, HEAD_DIM), pos),
            pl.BlockSpec((tm, HEAD_DIM), pos),
            pl.BlockSpec((tm, HEAD_DIM), pos),
        ],
        out_specs=[
            pl.BlockSpec((N_Q_HEADS, tm, HEAD_DIM), lambda i: (0, i, 0)),
            pl.BlockSpec((tm, KV_W), row),
            pl.BlockSpec((tm, 2 * KV_W), row),
            pl.BlockSpec((tm, ATTN_W), row),
            pl.BlockSpec((tm, D_MODEL), row),
        ],
        out_shape=[
            jax.ShapeDtypeStruct((N_Q_HEADS, n, HEAD_DIM), _BF16),
            jax.ShapeDtypeStruct((n, KV_W), _BF16),
            jax.ShapeDtypeStruct((n, 2 * KV_W), _BF16),
            jax.ShapeDtypeStruct((n, ATTN_W), _BF16),
            jax.ShapeDtypeStruct((n, D_MODEL), _BF16),
        ],
        compiler_params=pltpu.CompilerParams(
            dimension_semantics=("parallel",), vmem_limit_bytes=VMEM_LIMIT_BYTES),
        name="attn_proj",
    )(xf, nw, w_in, qw, kw, cos, s_lo, s_hi)


def _rest_proj_kernel(h_ref, *refs):
    w_refs = refs[:REST_BLOCKS]
    snw_ref, sw_ref, sb_ref, hc_ref, cb_ref, gc_ref, os_ref, gs_ref = refs[REST_BLOCKS:]
    tm = h_ref.shape[0]

    def proj(idx):
        return jnp.dot(h_ref[...], w_refs[idx][...], preferred_element_type=_F32)

    hc_ref[...] = proj(2) * proj(0)
    cb_ref[...] = proj(1).astype(_BF16)
    gc_ref[...] = _silu(proj(3)).astype(_BF16)
    gs_ref[...] = _silu(proj(6)).astype(_BF16)
    u = _gelu(proj(4))
    v = _gelu(proj(5))
    n_chunks = tm // CHUNK
    for g in range(SGU_GROUPS):
        lo = g * SGU_GC
        vn = (_rms(v[:, lo:lo + SGU_GC]) * snw_ref[:, lo:lo + SGU_GC]).astype(_BF16)
        rhs = jnp.concatenate([vn[c * CHUNK:(c + 1) * CHUNK, :] for c in range(n_chunks)], axis=1)
        s = jnp.dot(sw_ref[g], rhs, preferred_element_type=_F32) + sb_ref[:, g:g + 1]
        for c in range(n_chunks):
            os_ref[c * CHUNK:(c + 1) * CHUNK, lo:lo + SGU_GC] = (
                u[c * CHUNK:(c + 1) * CHUNK, lo:lo + SGU_GC] * s[:, c * SGU_GC:(c + 1) * SGU_GC]
            ).astype(_BF16)


def _rest_proj(h, w_in, layer, snw, sw, sb_t):
    n = h.shape[0]
    tm = PROJ_TM
    row = lambda i: (i, 0)
    const = lambda i: (0, 0)
    outs = [jax.ShapeDtypeStruct((n, REST_BLOCK_W), dt) for dt in (_F32, _BF16, _BF16, _BF16, _BF16)]
    first_block = ATTN_PROJ_W // REST_BLOCK_W
    w_specs = [
        pl.BlockSpec((None, D_MODEL, REST_BLOCK_W), (lambda i, j=j: (layer, 0, first_block + j)),
                     pipeline_mode=pl.Buffered(1))
        for j in range(REST_BLOCKS)
    ]
    return pl.pallas_call(
        _rest_proj_kernel,
        grid=(n // tm,),
        in_specs=[
            pl.BlockSpec((tm, D_MODEL), row),
            *w_specs,
            pl.BlockSpec((1, SGU_W), const),
            pl.BlockSpec((SGU_GROUPS, CHUNK, CHUNK), lambda i: (0, 0, 0)),
            pl.BlockSpec((CHUNK, SGU_GROUPS), const),
        ],
        out_specs=[pl.BlockSpec((tm, REST_BLOCK_W), row)] * 5,
        out_shape=outs,
        compiler_params=pltpu.CompilerParams(
            dimension_semantics=("parallel",), vmem_limit_bytes=VMEM_LIMIT_BYTES),
        name="rest_proj",
    )(h, *([w_in] * REST_BLOCKS), snw, sw, sb_t)


def _flash_online_kernel(q_ref, k_ref, v_ref, o_ref, s_scr, p_scr, m_scr, l_scr, acc_scr, *, tk, rb):
    tq = q_ref.shape[1]
    n_chunks = k_ref.shape[0] // tk

    m_scr[...] = jnp.full(m_scr.shape, -jnp.inf, _F32)
    l_scr[...] = jnp.zeros(l_scr.shape, _F32)
    acc_scr[...] = jnp.zeros(acc_scr.shape, _F32)
    p_scr[1] = jnp.zeros(p_scr.shape[1:], _BF16)

    def kv_rows(c):
        return pl.ds(pl.multiple_of(c * tk, tk), tk)

    def qk(g, c):
        return lax.dot_general(q_ref[g], k_ref[kv_rows(c), :], (((1,), (1,)), ((), ())),
                               preferred_element_type=_F32)

    def pv(g, slot, c):
        acc_scr[g] += jnp.dot(p_scr[slot], v_ref[kv_rows(c), :], preferred_element_type=_F32)

    def softmax(g, slot):
        for r in range(tq // rb):
            rows = slice(r * rb, (r + 1) * rb)
            s = s_scr[slot, rows, :]
            m_prev = m_scr[g, rows, :]
            m_next = jnp.maximum(m_prev, jnp.max(s, axis=1, keepdims=True))
            alpha = jnp.exp2(m_prev - m_next)
            p = jnp.exp2(s - jnp.tile(m_next, (1, tk // LANES)))
            l_scr[g, rows, :] = alpha * l_scr[g, rows, :] + jnp.sum(p, axis=1, keepdims=True)
            m_scr[g, rows, :] = m_next
            acc_scr[g, rows, :] = alpha * acc_scr[g, rows, :]
            p_scr[slot, rows, :] = p.astype(_BF16)

    s_scr[0] = qk(0, 0)

    def body(c, carry):
        for g in range(Q_PER_KV):
            slot = g % 2
            if g < Q_PER_KV - 1:
                s_scr[1 - slot] = qk(g + 1, c)
            else:
                s_scr[1 - slot] = qk(0, jnp.minimum(c + 1, n_chunks - 1))
            softmax(g, slot)
            if g == 0:
                pv(Q_PER_KV - 1, 1, jnp.maximum(c - 1, 0))
            else:
                pv(g - 1, 1 - slot, c)
        return carry

    lax.fori_loop(0, n_chunks, body, 0)
    pv(Q_PER_KV - 1, 1, n_chunks - 1)

    for g in range(Q_PER_KV):
        o_ref[:, g * HEAD_DIM:(g + 1) * HEAD_DIM] = (acc_scr[g] / l_scr[g]).astype(_BF16)


def _flash_online(q, k, v1, batch, seq_len):
    tq, tk = ATTN_TQ, ONLINE_TK
    q_tiles = seq_len // tq
    group_w = Q_PER_KV * HEAD_DIM
    kern = functools.partial(_flash_online_kernel, tk=tk, rb=SOFTMAX_ROWS)
    return pl.pallas_call(
        kern,
        grid=(batch, N_KV_HEADS, q_tiles),
        in_specs=[
            pl.BlockSpec((Q_PER_KV, tq, HEAD_DIM), lambda bi, hi, qi: (hi, bi * q_tiles + qi, 0)),
            pl.BlockSpec((seq_len, HEAD_DIM), lambda bi, hi, qi: (bi, hi)),
            pl.BlockSpec((seq_len, HEAD_DIM), lambda bi, hi, qi: (bi, 2 * hi)),
        ],
        out_specs=pl.BlockSpec((tq, group_w), lambda bi, hi, qi: (bi * q_tiles + qi, hi)),
        out_shape=jax.ShapeDtypeStruct((batch * seq_len, ATTN_W), _BF16),
        scratch_shapes=[
            pltpu.VMEM((2, tq, tk), _F32),
            pltpu.VMEM((2, tq, tk), _BF16),
            pltpu.VMEM((Q_PER_KV, tq, LANES), _F32),
            pltpu.VMEM((Q_PER_KV, tq, LANES), _F32),
            pltpu.VMEM((Q_PER_KV, tq, HEAD_DIM), _F32),
        ],
        compiler_params=pltpu.CompilerParams(
            dimension_semantics=("parallel", "parallel", "parallel"),
            vmem_limit_bytes=VMEM_LIMIT_BYTES),
        name="flash_online",
    )(q, k, v1)


def _flash_bounded_kernel(shift_ref, q_ref, k_ref, v_ref, o_ref, p_scr, acc_scr, *, tk):
    n_chunks = k_ref.shape[0] // tk
    shift = jnp.tile(shift_ref[...], (1, tk // LANES))

    def kv_rows(c):
        return pl.ds(pl.multiple_of(c * tk, tk), tk)

    def probs(g, c, slot):
        s = lax.dot_general(q_ref[g], k_ref[kv_rows(c), :], (((1,), (1,)), ((), ())),
                            preferred_element_type=_F32)
        p_scr[slot] = jnp.exp2(s - shift).astype(_BF16)

    def pv(g, slot, c, first):
        d = jnp.dot(p_scr[slot], v_ref[kv_rows(c), :], preferred_element_type=_F32)
        if first:
            acc_scr[g] = d
        else:
            acc_scr[g] += d

    def chunk(c, first=False, last=False):
        for g in range(Q_PER_KV):
            slot = g % 2
            if g < Q_PER_KV - 1:
                probs(g + 1, c, 1 - slot)
            elif not last:
                probs(0, c + 1, 1 - slot)
            pv(g, slot, c, first)

    probs(0, 0, 0)
    chunk(0, first=True)

    def body(c, carry):
        chunk(c)
        return carry

    lax.fori_loop(1, n_chunks - 1, body, 0)
    chunk(n_chunks - 1, last=True)

    for g in range(Q_PER_KV):
        a = acc_scr[g]
        o_ref[:, g * HEAD_DIM:(g + 1) * HEAD_DIM] = (a[:, :HEAD_DIM] / a[:, HEAD_DIM:]).astype(_BF16)


def _flash_bounded(q, k, v1, shift, batch, seq_len):
    tq, tk = ATTN_TQ, BOUNDED_TK
    assert seq_len // tk >= 2
    q_tiles = seq_len // tq
    group_w = Q_PER_KV * HEAD_DIM
    kern = functools.partial(_flash_bounded_kernel, tk=tk)
    return pl.pallas_call(
        kern,
        grid=(batch, N_KV_HEADS, q_tiles),
        in_specs=[
            pl.BlockSpec((1, LANES), lambda bi, hi, qi: (0, 0)),
            pl.BlockSpec((Q_PER_KV, tq, HEAD_DIM), lambda bi, hi, qi: (hi, bi * q_tiles + qi, 0)),
            pl.BlockSpec((seq_len, HEAD_DIM), lambda bi, hi, qi: (bi, hi)),
            pl.BlockSpec((seq_len, 2 * HEAD_DIM), lambda bi, hi, qi: (bi, hi)),
        ],
        out_specs=pl.BlockSpec((tq, group_w), lambda bi, hi, qi: (bi * q_tiles + qi, hi)),
        out_shape=jax.ShapeDtypeStruct((batch * seq_len, ATTN_W), _BF16),
        scratch_shapes=[
            pltpu.VMEM((2, tq, tk), _BF16),
            pltpu.VMEM((Q_PER_KV, tq, 2 * HEAD_DIM), _F32),
        ],
        compiler_params=pltpu.CompilerParams(
            dimension_semantics=("parallel", "parallel", "parallel"),
            vmem_limit_bytes=VMEM_LIMIT_BYTES),
        name="flash_bounded",
    )(shift, q, k, v1)


def _score_bound(q_norm_w, k_norm_w):
    rounding = (1.0 + 2.0 ** -8) ** 2
    return (HEAD_DIM * QK_SCALE_LOG2 * rounding) * jnp.max(jnp.abs(q_norm_w)) * jnp.max(jnp.abs(k_norm_w))


def _attention(q, k, v1, q_norm_w, k_norm_w, batch, seq_len):
    shift = _score_bound(q_norm_w, k_norm_w)
    return lax.cond(
        shift <= MAX_BOUNDED_SHIFT,
        lambda: _flash_bounded(q, k, v1, jnp.full((1, LANES), shift, _F32), batch, seq_len),
        lambda: _flash_online(q, k, v1, batch, seq_len))


def _mix_kernel(x_ref, oa_ref, ga_ref, hc_ref, hp_ref, hn_ref, cb_ref, gc_ref, os_ref, gs_ref,
                cw_ref, bw_ref, w_ref, fw_ref, out_ref, mixed_scr, *, tiles_per_seq, final_norm):
    i = pl.program_id(0)
    tm = x_ref.shape[0]
    f32 = lambda ref: ref[...].astype(_F32)
    mixed_scr[:, :ATTN_W] = (_rms(f32(oa_ref)) * bw_ref[:, :ATTN_W] * f32(ga_ref)).astype(_BF16)

    hc = hc_ref[...]
    t = i % tiles_per_seq
    prev_row = jnp.where(t == 0, 0.0, hp_ref[SUBLANES - 1:SUBLANES, :])
    next_row = jnp.where(t == tiles_per_seq - 1, 0.0, hn_ref[0:1, :])
    ridx = lax.broadcasted_iota(jnp.int32, hc.shape, 0)
    up = jnp.where(ridx == 0, prev_row, pltpu.roll(hc, 1, 0))
    down = jnp.where(ridx == tm - 1, next_row, pltpu.roll(hc, tm - 1, 0))
    y = f32(cb_ref) * (up * cw_ref[0:1, :] + hc * cw_ref[1:2, :] + down * cw_ref[2:3, :])
    c0 = ATTN_W
    mixed_scr[:, c0:c0 + CONV_W] = (_rms(y) * bw_ref[:, c0:c0 + CONV_W] * f32(gc_ref)).astype(_BF16)
    c1 = ATTN_W + CONV_W
    mixed_scr[:, c1:] = (_rms(f32(os_ref)) * bw_ref[:, c1:] * f32(gs_ref)).astype(_BF16)

    out = x_ref[...] + jnp.dot(mixed_scr[...], w_ref[...], preferred_element_type=_F32)
    if final_norm:
        out = _rms(out) * fw_ref[...]
    out_ref[...] = out


def _mix(xf, oa, ga, hc, cb, gc, osg, gs, cw_t, bw, w_out, layer, fw, seq_len, final_norm):
    n = xf.shape[0]
    tm = MIX_TM
    tiles_per_seq = seq_len // tm
    halo_blocks = tm // SUBLANES
    n_halo = n // SUBLANES
    row = lambda i: (i, 0)
    const = lambda i: (0, 0)
    prev = lambda i: (jnp.maximum(i * halo_blocks - 1, 0), 0)
    nxt = lambda i: (jnp.minimum((i + 1) * halo_blocks, n_halo - 1), 0)
    kern = functools.partial(_mix_kernel, tiles_per_seq=tiles_per_seq, final_norm=final_norm)
    return pl.pallas_call(
        kern,
        grid=(n // tm,),
        in_specs=[
            pl.BlockSpec((tm, D_MODEL), row),
            pl.BlockSpec((tm, ATTN_W), row),
            pl.BlockSpec((tm, ATTN_W), row),
            pl.BlockSpec((tm, CONV_W), row),
            pl.BlockSpec((SUBLANES, CONV_W), prev),
            pl.BlockSpec((SUBLANES, CONV_W), nxt),
            pl.BlockSpec((tm, CONV_W), row),
            pl.BlockSpec((tm, CONV_W), row),
            pl.BlockSpec((tm, SGU_W), row),
            pl.BlockSpec((tm, SGU_W), row),
            pl.BlockSpec((3, CONV_W), const),
            pl.BlockSpec((1, MIX_W), const),
            pl.BlockSpec((None, MIX_W, D_MODEL), lambda i: (layer, 0, 0), pipeline_mode=pl.Buffered(1)),
            pl.BlockSpec((1, D_MODEL), const),
        ],
        out_specs=pl.BlockSpec((tm, D_MODEL), row),
        out_shape=jax.ShapeDtypeStruct((n, D_MODEL), _F32),
        scratch_shapes=[pltpu.VMEM((tm, MIX_W), _BF16)],
        compiler_params=pltpu.CompilerParams(
            dimension_semantics=("parallel",), vmem_limit_bytes=VMEM_LIMIT_BYTES),
        name="mix_out",
    )(xf, oa, ga, hc, hc, hc, cb, gc, osg, gs, cw_t, bw, w_out, fw)


def kernel(x, norm_w, w_in, q_norm_w, k_norm_w, conv_w, sgu_norm_w, sgu_w, sgu_b, branch_norm_w, w_out,
           final_norm_w):
    b, s, d = x.shape
    depth = w_in.shape[0]
    n = b * s
    cos, s_lo, s_hi = _rope_tables(s)
    sw = sgu_w.astype(_BF16)
    xf = x.reshape(n, d)
    for l in range(depth):
        q, k, v1, ga, h = _attn_proj(xf, norm_w[l].reshape(1, d), w_in, l, q_norm_w[l].reshape(1, HEAD_DIM),
                                     k_norm_w[l].reshape(1, HEAD_DIM), cos, s_lo, s_hi, s)
        hc, cb, gc, osg, gs = _rest_proj(h, w_in, l, sgu_norm_w[l].reshape(1, SGU_W), sw[l], sgu_b[l].T)
        oa = _attention(q, k, v1, q_norm_w[l], k_norm_w[l], b, s)
        xf = _mix(xf, oa, ga, hc, cb, gc, osg, gs, conv_w[l].T, branch_norm_w[l].reshape(1, MIX_W),
                  w_out, l, final_norm_w.reshape(1, d), s, final_norm=(l == depth - 1))
    return xf.reshape(b, s, d)
```

```python
import functools

import jax
import jax.numpy as jnp
import numpy as np
from jax import lax
from jax.experimental import pallas as pl
from jax.experimental.pallas import tpu as pltpu

D_MODEL = 2048
HEAD_DIM = 128
N_Q_HEADS = 8
N_KV_HEADS = 2
Q_PER_KV = N_Q_HEADS // N_KV_HEADS
ATTN_W = N_Q_HEADS * HEAD_DIM
KV_W = N_KV_HEADS * HEAD_DIM
CONV_W = 512
SGU_GROUPS = 4
SGU_GC = 128
SGU_W = SGU_GROUPS * SGU_GC
CHUNK = 128
GRID_W = 64
ROPE_THETA = 10000.0
AXIS_ROT = HEAD_DIM // 2
EPS = 1e-6
MIX_W = ATTN_W + CONV_W + SGU_W
ATTN_PROJ_W = 2 * ATTN_W + 2 * KV_W
REST_BLOCK_W = 512
REST_BLOCKS = 7
assert CONV_W == SGU_W == REST_BLOCK_W and ATTN_PROJ_W % REST_BLOCK_W == 0
QK_SCALE_LOG2 = float(HEAD_DIM ** -0.5 * np.log2(np.e))

LANES = 128
SUBLANES = 8
VMEM_LIMIT_BYTES = 56 * 1024 * 1024

PROJ_TM = 512
MIX_TM = 512
BOUNDED_TQ = 1024
ONLINE_TQ = 512
BOUNDED_TK = 2048
ONLINE_TK = 512
SOFTMAX_ROWS = 64
assert Q_PER_KV % 2 == 0

MAX_BOUNDED_SHIFT = 48.0

_BF16 = jnp.bfloat16
_F32 = jnp.float32


def _rms(t):
    return t * lax.rsqrt(jnp.mean(t * t, axis=-1, keepdims=True) + EPS)


def _silu(t):
    return t * (1.0 / (1.0 + jnp.exp(-t)))


def _gelu(t):
    return 0.5 * t * (1.0 + lax.erf(t * np.float32(np.sqrt(0.5))))


def _rope_tables(seq_len):
    rows = seq_len // GRID_W
    row = jnp.repeat(jnp.arange(rows, dtype=_F32), GRID_W)
    col = jnp.tile(jnp.arange(GRID_W, dtype=_F32), rows)
    inv_freq = ROPE_THETA ** (-jnp.arange(0, AXIS_ROT, 2, dtype=_F32) / AXIS_ROT)
    ang = jnp.stack([row, col], axis=-1)[:, :, None] * inv_freq
    ang = jnp.broadcast_to(ang[:, :, None, :], (seq_len, 2, 2, AXIS_ROT // 2)).reshape(seq_len, HEAD_DIM)
    cos, sin = jnp.cos(ang), jnp.sin(ang)
    first_half = (jnp.arange(HEAD_DIM) % AXIS_ROT) < (AXIS_ROT // 2)
    sin_lo = jnp.where(first_half, -sin, 0.0)
    sin_hi = jnp.where(first_half, 0.0, sin)
    return cos, sin_lo, sin_hi


def _attn_proj_kernel(x_ref, nw_ref, w_ref, qw_ref, kw_ref, cos_ref, slo_ref, shi_ref,
                      q_ref, k_ref, v_ref, g_ref, h_ref):
    tm = x_ref.shape[0]
    half = tm // 2
    qw, kw = qw_ref[...], kw_ref[...]
    pair = 2 * HEAD_DIM
    g0 = ATTN_W + 2 * KV_W
    ones = jnp.ones((half, HEAD_DIM), _BF16)
    for hf in range(2):
        rows = slice(hf * half, (hf + 1) * half)
        h_ref[rows, :] = (_rms(x_ref[rows, :]) * nw_ref[...]).astype(_BF16)
        cos, s_lo, s_hi = cos_ref[rows, :], slo_ref[rows, :], shi_ref[rows, :]

        def norm_rope(t, w):
            tn = _rms(t) * w
            return (tn * cos + pltpu.roll(tn, HEAD_DIM - AXIS_ROT // 2, 1) * s_lo
                    + pltpu.roll(tn, AXIS_ROT // 2, 1) * s_hi)

        def proj(lo, width):
            return jnp.dot(h_ref[rows, :], w_ref[:, lo:lo + width], preferred_element_type=_F32)

        for c in range(ATTN_W // pair):
            acc = proj(c * pair, pair)
            for hh in range(2):
                t = norm_rope(acc[:, hh * HEAD_DIM:(hh + 1) * HEAD_DIM], qw) * QK_SCALE_LOG2
                q_ref[2 * c + hh, rows, :] = t.astype(_BF16)
        acc = proj(ATTN_W, KV_W)
        for hh in range(N_KV_HEADS):
            t = norm_rope(acc[:, hh * HEAD_DIM:(hh + 1) * HEAD_DIM], kw)
            k_ref[rows, hh * HEAD_DIM:(hh + 1) * HEAD_DIM] = t.astype(_BF16)
        acc = proj(ATTN_W + KV_W, KV_W)
        for hh in range(N_KV_HEADS):
            v_ref[rows, 2 * hh * HEAD_DIM:(2 * hh + 1) * HEAD_DIM] = (
                acc[:, hh * HEAD_DIM:(hh + 1) * HEAD_DIM].astype(_BF16))
            v_ref[rows, (2 * hh + 1) * HEAD_DIM:(2 * hh + 2) * HEAD_DIM] = ones
        for c in range(ATTN_W // 512):
            g_ref[rows, c * 512:(c + 1) * 512] = _silu(proj(g0 + c * 512, 512)).astype(_BF16)


def _attn_proj(xf, nw, w_in, layer, qw, kw, cos, s_lo, s_hi, seq_len):
    n = xf.shape[0]
    tm = PROJ_TM
    tiles_per_seq = seq_len // tm
    row = lambda i: (i, 0)
    const = lambda i: (0, 0)
    pos = lambda i: (i % tiles_per_seq, 0)
    return pl.pallas_call(
        _attn_proj_kernel,
        grid=(n // tm,),
        in_specs=[
            pl.BlockSpec((tm, D_MODEL), row),
            pl.BlockSpec((1, D_MODEL), const),
            pl.BlockSpec((None, D_MODEL, ATTN_PROJ_W), lambda i: (layer, 0, 0), pipeline_mode=pl.Buffered(1)),
            pl.BlockSpec((1, HEAD_DIM), const),
            pl.BlockSpec((1, HEAD_DIM), const),
            pl.BlockSpec((tm, HEAD_DIM), pos),
            pl.BlockSpec((tm, HEAD_DIM), pos),
            pl.BlockSpec((tm, HEAD_DIM), pos),
        ],
        out_specs=[
            pl.BlockSpec((N_Q_HEADS, tm, HEAD_DIM), lambda i: (0, i, 0)),
            pl.BlockSpec((tm, KV_W), row),
            pl.BlockSpec((tm, 2 * KV_W), row),
            pl.BlockSpec((tm, ATTN_W), row),
            pl.BlockSpec((tm, D_MODEL), row),
        ],
        out_shape=[
            jax.ShapeDtypeStruct((N_Q_HEADS, n, HEAD_DIM), _BF16),
            jax.ShapeDtypeStruct((n, KV_W), _BF16),
            jax.ShapeDtypeStruct((n, 2 * KV_W), _BF16),
            jax.ShapeDtypeStruct((n, ATTN_W), _BF16),
            jax.ShapeDtypeStruct((n, D_MODEL), _BF16),
        ],
        compiler_params=pltpu.CompilerParams(
            dimension_semantics=("parallel",), vmem_limit_bytes=VMEM_LIMIT_BYTES),
        name="attn_proj",
    )(xf, nw, w_in, qw, kw, cos, s_lo, s_hi)


def _rest_proj_kernel(h_ref, *refs):
    w_refs = refs[:REST_BLOCKS]
    snw_ref, sw_ref, sb_ref, hc_ref, cb_ref, gc_ref, os_ref, gs_ref = refs[REST_BLOCKS:]
    tm = h_ref.shape[0]

    def proj(idx):
        return jnp.dot(h_ref[...], w_refs[idx][...], preferred_element_type=_F32)

    hc_ref[...] = proj(2) * proj(0)
    cb_ref[...] = proj(1).astype(_BF16)
    gc_ref[...] = _silu(proj(3)).astype(_BF16)
    gs_ref[...] = _silu(proj(6)).astype(_BF16)
    u = _gelu(proj(4))
    v = _gelu(proj(5))
    n_chunks = tm // CHUNK
    for g in range(SGU_GROUPS):
        lo = g * SGU_GC
        vn = (_rms(v[:, lo:lo + SGU_GC]) * snw_ref[:, lo:lo + SGU_GC]).astype(_BF16)
        rhs = jnp.concatenate([vn[c * CHUNK:(c + 1) * CHUNK, :] for c in range(n_chunks)], axis=1)
        s = jnp.dot(sw_ref[g], rhs, preferred_element_type=_F32) + sb_ref[:, g:g + 1]
        for c in range(n_chunks):
            os_ref[c * CHUNK:(c + 1) * CHUNK, lo:lo + SGU_GC] = (
                u[c * CHUNK:(c + 1) * CHUNK, lo:lo + SGU_GC] * s[:, c * SGU_GC:(c + 1) * SGU_GC]
            ).astype(_BF16)


def _rest_proj(h, w_in, layer, snw, sw, sb_t):
    n = h.shape[0]
    tm = PROJ_TM
    row = lambda i: (i, 0)
    const = lambda i: (0, 0)
    outs = [jax.ShapeDtypeStruct((n, REST_BLOCK_W), dt) for dt in (_F32, _BF16, _BF16, _BF16, _BF16)]
    first_block = ATTN_PROJ_W // REST_BLOCK_W
    w_specs = [
        pl.BlockSpec((None, D_MODEL, REST_BLOCK_W), (lambda i, j=j: (layer, 0, first_block + j)),
                     pipeline_mode=pl.Buffered(1))
        for j in range(REST_BLOCKS)
    ]
    return pl.pallas_call(
        _rest_proj_kernel,
        grid=(n // tm,),
        in_specs=[
            pl.BlockSpec((tm, D_MODEL), row),
            *w_specs,
            pl.BlockSpec((1, SGU_W), const),
            pl.BlockSpec((SGU_GROUPS, CHUNK, CHUNK), lambda i: (0, 0, 0)),
            pl.BlockSpec((CHUNK, SGU_GROUPS), const),
        ],
        out_specs=[pl.BlockSpec((tm, REST_BLOCK_W), row)] * 5,
        out_shape=outs,
        compiler_params=pltpu.CompilerParams(
            dimension_semantics=("parallel",), vmem_limit_bytes=VMEM_LIMIT_BYTES),
        name="rest_proj",
    )(h, *([w_in] * REST_BLOCKS), snw, sw, sb_t)


def _flash_online_kernel(q_ref, k_ref, v_ref, o_ref, s_scr, p_scr, m_scr, l_scr, acc_scr, *, tk, rb):
    tq = q_ref.shape[1]
    n_chunks = k_ref.shape[0] // tk

    m_scr[...] = jnp.full(m_scr.shape, -jnp.inf, _F32)
    l_scr[...] = jnp.zeros(l_scr.shape, _F32)
    acc_scr[...] = jnp.zeros(acc_scr.shape, _F32)
    p_scr[1] = jnp.zeros(p_scr.shape[1:], _BF16)

    def kv_rows(c):
        return pl.ds(pl.multiple_of(c * tk, tk), tk)

    def qk(g, c):
        return lax.dot_general(q_ref[g], k_ref[kv_rows(c), :], (((1,), (1,)), ((), ())),
                               preferred_element_type=_F32)

    def pv(g, slot, c):
        acc_scr[g] += jnp.dot(p_scr[slot], v_ref[kv_rows(c), :], preferred_element_type=_F32)

    def softmax(g, slot):
        for r in range(tq // rb):
            rows = slice(r * rb, (r + 1) * rb)
            s = s_scr[slot, rows, :]
            m_prev = m_scr[g, rows, :]
            m_next = jnp.maximum(m_prev, jnp.max(s, axis=1, keepdims=True))
            alpha = jnp.exp2(m_prev - m_next)
            p = jnp.exp2(s - jnp.tile(m_next, (1, tk // LANES)))
            l_scr[g, rows, :] = alpha * l_scr[g, rows, :] + jnp.sum(p, axis=1, keepdims=True)
            m_scr[g, rows, :] = m_next
            acc_scr[g, rows, :] = alpha * acc_scr[g, rows, :]
            p_scr[slot, rows, :] = p.astype(_BF16)

    s_scr[0] = qk(0, 0)

    def body(c, carry):
        for g in range(Q_PER_KV):
            slot = g % 2
            if g < Q_PER_KV - 1:
                s_scr[1 - slot] = qk(g + 1, c)
            else:
                s_scr[1 - slot] = qk(0, jnp.minimum(c + 1, n_chunks - 1))
            softmax(g, slot)
            if g == 0:
                pv(Q_PER_KV - 1, 1, jnp.maximum(c - 1, 0))
            else:
                pv(g - 1, 1 - slot, c)
        return carry

    lax.fori_loop(0, n_chunks, body, 0)
    pv(Q_PER_KV - 1, 1, n_chunks - 1)

    for g in range(Q_PER_KV):
        o_ref[:, g * HEAD_DIM:(g + 1) * HEAD_DIM] = (acc_scr[g] / l_scr[g]).astype(_BF16)


def _flash_online(q, k, v1, batch, seq_len):
    tq, tk = ONLINE_TQ, ONLINE_TK
    q_tiles = seq_len // tq
    group_w = Q_PER_KV * HEAD_DIM
    kern = functools.partial(_flash_online_kernel, tk=tk, rb=SOFTMAX_ROWS)
    return pl.pallas_call(
        kern,
        grid=(batch, N_KV_HEADS, q_tiles),
        in_specs=[
            pl.BlockSpec((Q_PER_KV, tq, HEAD_DIM), lambda bi, hi, qi: (hi, bi * q_tiles + qi, 0)),
            pl.BlockSpec((seq_len, HEAD_DIM), lambda bi, hi, qi: (bi, hi)),
            pl.BlockSpec((seq_len, HEAD_DIM), lambda bi, hi, qi: (bi, 2 * hi)),
        ],
        out_specs=pl.BlockSpec((tq, group_w), lambda bi, hi, qi: (bi * q_tiles + qi, hi)),
        out_shape=jax.ShapeDtypeStruct((batch * seq_len, ATTN_W), _BF16),
        scratch_shapes=[
            pltpu.VMEM((2, tq, tk), _F32),
            pltpu.VMEM((2, tq, tk), _BF16),
            pltpu.VMEM((Q_PER_KV, tq, LANES), _F32),
            pltpu.VMEM((Q_PER_KV, tq, LANES), _F32),
            pltpu.VMEM((Q_PER_KV, tq, HEAD_DIM), _F32),
        ],
        compiler_params=pltpu.CompilerParams(
            dimension_semantics=("parallel", "parallel", "parallel"),
            vmem_limit_bytes=VMEM_LIMIT_BYTES),
        name="flash_online",
    )(q, k, v1)


def _flash_bounded_kernel(shift_ref, q_ref, k_ref, v_ref, o_ref, p_scr, acc_scr, *, tk):
    n_chunks = k_ref.shape[0] // tk
    shift = jnp.tile(shift_ref[...], (1, tk // LANES))

    def kv_rows(c):
        return pl.ds(pl.multiple_of(c * tk, tk), tk)

    def probs(g, c, slot):
        s = lax.dot_general(q_ref[g], k_ref[kv_rows(c), :], (((1,), (1,)), ((), ())),
                            preferred_element_type=_F32)
        p_scr[slot] = jnp.exp2(s - shift).astype(_BF16)

    def pv(g, slot, c, first):
        d = jnp.dot(p_scr[slot], v_ref[kv_rows(c), :], preferred_element_type=_F32)
        if first:
            acc_scr[g] = d
        else:
            acc_scr[g] += d

    def chunk(c, first=False, last=False):
        for g in range(Q_PER_KV):
            slot = g % 2
            if g < Q_PER_KV - 1:
                probs(g + 1, c, 1 - slot)
            elif not last:
                probs(0, c + 1, 1 - slot)
            pv(g, slot, c, first)

    probs(0, 0, 0)
    chunk(0, first=True)

    def body(c, carry):
        chunk(c)
        return carry

    lax.fori_loop(1, n_chunks - 1, body, 0)
    chunk(n_chunks - 1, last=True)

    for g in range(Q_PER_KV):
        a = acc_scr[g]
        o_ref[:, g * HEAD_DIM:(g + 1) * HEAD_DIM] = (a[:, :HEAD_DIM] / a[:, HEAD_DIM:]).astype(_BF16)


def _flash_bounded(q, k, v1, shift, batch, seq_len):
    tq, tk = BOUNDED_TQ, BOUNDED_TK
    assert seq_len // tk >= 2
    q_tiles = seq_len // tq
    group_w = Q_PER_KV * HEAD_DIM
    kern = functools.partial(_flash_bounded_kernel, tk=tk)
    return pl.pallas_call(
        kern,
        grid=(batch, N_KV_HEADS, q_tiles),
        in_specs=[
            pl.BlockSpec((1, LANES), lambda bi, hi, qi: (0, 0)),
            pl.BlockSpec((Q_PER_KV, tq, HEAD_DIM), lambda bi, hi, qi: (hi, bi * q_tiles + qi, 0)),
            pl.BlockSpec((seq_len, HEAD_DIM), lambda bi, hi, qi: (bi, hi)),
            pl.BlockSpec((seq_len, 2 * HEAD_DIM), lambda bi, hi, qi: (bi, hi)),
        ],
        out_specs=pl.BlockSpec((tq, group_w), lambda bi, hi, qi: (bi * q_tiles + qi, hi)),
        out_shape=jax.ShapeDtypeStruct((batch * seq_len, ATTN_W), _BF16),
        scratch_shapes=[
            pltpu.VMEM((2, tq, tk), _BF16),
            pltpu.VMEM((Q_PER_KV, tq, 2 * HEAD_DIM), _F32),
        ],
        compiler_params=pltpu.CompilerParams(
            dimension_semantics=("parallel", "parallel", "parallel"),
            vmem_limit_bytes=VMEM_LIMIT_BYTES),
        name="flash_bounded",
    )(shift, q, k, v1)


def _score_bound(q_norm_w, k_norm_w):
    rounding = (1.0 + 2.0 ** -8) ** 2
    return (HEAD_DIM * QK_SCALE_LOG2 * rounding) * jnp.max(jnp.abs(q_norm_w)) * jnp.max(jnp.abs(k_norm_w))


def _attention(q, k, v1, q_norm_w, k_norm_w, batch, seq_len):
    shift = _score_bound(q_norm_w, k_norm_w)
    return lax.cond(
        shift <= MAX_BOUNDED_SHIFT,
        lambda: _flash_bounded(q, k, v1, jnp.full((1, LANES), shift, _F32), batch, seq_len),
        lambda: _flash_online(q, k, v1, batch, seq_len))


def _mix_kernel(x_ref, oa_ref, ga_ref, hc_ref, hp_ref, hn_ref, cb_ref, gc_ref, os_ref, gs_ref,
                cw_ref, bw_ref, w_ref, fw_ref, out_ref, mixed_scr, *, tiles_per_seq, final_norm):
    i = pl.program_id(0)
    tm = x_ref.shape[0]
    f32 = lambda ref: ref[...].astype(_F32)
    mixed_scr[:, :ATTN_W] = (_rms(f32(oa_ref)) * bw_ref[:, :ATTN_W] * f32(ga_ref)).astype(_BF16)

    hc = hc_ref[...]
    t = i % tiles_per_seq
    prev_row = jnp.where(t == 0, 0.0, hp_ref[SUBLANES - 1:SUBLANES, :])
    next_row = jnp.where(t == tiles_per_seq - 1, 0.0, hn_ref[0:1, :])
    ridx = lax.broadcasted_iota(jnp.int32, hc.shape, 0)
    up = jnp.where(ridx == 0, prev_row, pltpu.roll(hc, 1, 0))
    down = jnp.where(ridx == tm - 1, next_row, pltpu.roll(hc, tm - 1, 0))
    y = f32(cb_ref) * (up * cw_ref[0:1, :] + hc * cw_ref[1:2, :] + down * cw_ref[2:3, :])
    c0 = ATTN_W
    mixed_scr[:, c0:c0 + CONV_W] = (_rms(y) * bw_ref[:, c0:c0 + CONV_W] * f32(gc_ref)).astype(_BF16)
    c1 = ATTN_W + CONV_W
    mixed_scr[:, c1:] = (_rms(f32(os_ref)) * bw_ref[:, c1:] * f32(gs_ref)).astype(_BF16)

    out = x_ref[...] + jnp.dot(mixed_scr[...], w_ref[...], preferred_element_type=_F32)
    if final_norm:
        out = _rms(out) * fw_ref[...]
    out_ref[...] = out


def _mix(xf, oa, ga, hc, cb, gc, osg, gs, cw_t, bw, w_out, layer, fw, seq_len, final_norm):
    n = xf.shape[0]
    tm = MIX_TM
    tiles_per_seq = seq_len // tm
    halo_blocks = tm // SUBLANES
    n_halo = n // SUBLANES
    row = lambda i: (i, 0)
    const = lambda i: (0, 0)
    prev = lambda i: (jnp.maximum(i * halo_blocks - 1, 0), 0)
    nxt = lambda i: (jnp.minimum((i + 1) * halo_blocks, n_halo - 1), 0)
    kern = functools.partial(_mix_kernel, tiles_per_seq=tiles_per_seq, final_norm=final_norm)
    return pl.pallas_call(
        kern,
        grid=(n // tm,),
        in_specs=[
            pl.BlockSpec((tm, D_MODEL), row),
            pl.BlockSpec((tm, ATTN_W), row),
            pl.BlockSpec((tm, ATTN_W), row),
            pl.BlockSpec((tm, CONV_W), row),
            pl.BlockSpec((SUBLANES, CONV_W), prev),
            pl.BlockSpec((SUBLANES, CONV_W), nxt),
            pl.BlockSpec((tm, CONV_W), row),
            pl.BlockSpec((tm, CONV_W), row),
            pl.BlockSpec((tm, SGU_W), row),
            pl.BlockSpec((tm, SGU_W), row),
            pl.BlockSpec((3, CONV_W), const),
            pl.BlockSpec((1, MIX_W), const),
            pl.BlockSpec((None, MIX_W, D_MODEL), lambda i: (layer, 0, 0), pipeline_mode=pl.Buffered(1)),
            pl.BlockSpec((1, D_MODEL), const),
        ],
        out_specs=pl.BlockSpec((tm, D_MODEL), row),
        out_shape=jax.ShapeDtypeStruct((n, D_MODEL), _F32),
        scratch_shapes=[pltpu.VMEM((tm, MIX_W), _BF16)],
        compiler_params=pltpu.CompilerParams(
            dimension_semantics=("parallel",), vmem_limit_bytes=VMEM_LIMIT_BYTES),
        name="mix_out",
    )(xf, oa, ga, hc, hc, hc, cb, gc, osg, gs, cw_t, bw, w_out, fw)


def kernel(x, norm_w, w_in, q_norm_w, k_norm_w, conv_w, sgu_norm_w, sgu_w, sgu_b, branch_norm_w, w_out,
           final_norm_w):
    b, s, d = x.shape
    depth = w_in.shape[0]
    n = b * s
    cos, s_lo, s_hi = _rope_tables(s)
    sw = sgu_w.astype(_BF16)
    xf = x.reshape(n, d)
    for l in range(depth):
        q, k, v1, ga, h = _attn_proj(xf, norm_w[l].reshape(1, d), w_in, l, q_norm_w[l].reshape(1, HEAD_DIM),
                                     k_norm_w[l].reshape(1, HEAD_DIM), cos, s_lo, s_hi, s)
        hc, cb, gc, osg, gs = _rest_proj(h, w_in, l, sgu_norm_w[l].reshape(1, SGU_W), sw[l], sgu_b[l].T)
        oa = _attention(q, k, v1, q_norm_w[l], k_norm_w[l], b, s)
        xf = _mix(xf, oa, ga, hc, cb, gc, osg, gs, conv_w[l].T, branch_norm_w[l].reshape(1, MIX_W),
                  w_out, l, final_norm_w.reshape(1, d), s, final_norm=(l == depth - 1))
    return xf.reshape(b, s, d)
```

```python
import functools

import jax
import jax.numpy as jnp
import numpy as np
from jax import lax
from jax.experimental import pallas as pl
from jax.experimental.pallas import tpu as pltpu

D_MODEL = 2048
HEAD_DIM = 128
N_Q_HEADS = 8
N_KV_HEADS = 2
Q_PER_KV = N_Q_HEADS // N_KV_HEADS
ATTN_W = N_Q_HEADS * HEAD_DIM
KV_W = N_KV_HEADS * HEAD_DIM
CONV_W = 512
SGU_GROUPS = 4
SGU_GC = 128
SGU_W = SGU_GROUPS * SGU_GC
CHUNK = 128
GRID_W = 64
ROPE_THETA = 10000.0
AXIS_ROT = HEAD_DIM // 2
EPS = 1e-6
MIX_W = ATTN_W + CONV_W + SGU_W
ATTN_PROJ_W = 2 * ATTN_W + 2 * KV_W
REST_BLOCK_W = 512
REST_BLOCKS = 7
assert CONV_W == SGU_W == REST_BLOCK_W and ATTN_PROJ_W % REST_BLOCK_W == 0
QK_SCALE_LOG2 = float(HEAD_DIM ** -0.5 * np.log2(np.e))

LANES = 128
SUBLANES = 8
VMEM_LIMIT_BYTES = 56 * 1024 * 1024

PROJ_TM = 512
MIX_TM = 512
BOUNDED_TQ = 1024
ONLINE_TQ = 512
BOUNDED_TK = 2048
ONLINE_TK = 512
SOFTMAX_ROWS = 64
assert Q_PER_KV % 2 == 0

MAX_BOUNDED_SHIFT = 48.0

_BF16 = jnp.bfloat16
_F32 = jnp.float32


def _rms(t):
    return t * lax.rsqrt(jnp.mean(t * t, axis=-1, keepdims=True) + EPS)


def _silu(t):
    return t * (1.0 / (1.0 + jnp.exp(-t)))


def _gelu(t):
    return 0.5 * t * (1.0 + lax.erf(t * np.float32(np.sqrt(0.5))))


def _rope_tables(seq_len):
    inv_freq = ROPE_THETA ** (-jnp.arange(0, AXIS_ROT, 2, dtype=_F32) / AXIS_ROT)
    first_half = jnp.arange(AXIS_ROT) < (AXIS_ROT // 2)

    def axis_table(count):
        ang = jnp.arange(count, dtype=_F32)[:, None] * inv_freq
        ang = jnp.concatenate([ang, ang], axis=1)
        cos, sin = jnp.cos(ang), jnp.sin(ang)
        return jnp.stack([cos, jnp.where(first_half, -sin, 0.0), jnp.where(first_half, 0.0, sin)])

    row_t, col_t = axis_table(seq_len // GRID_W), axis_table(GRID_W)
    rope_row = jnp.concatenate([row_t, jnp.zeros_like(row_t)], axis=-1)
    rope_col = jnp.concatenate([jnp.zeros_like(col_t), col_t], axis=-1)
    return rope_row, rope_col


def _attn_proj_kernel(x_ref, nw_ref, w_ref, qw_ref, kw_ref, rrow_ref, rcol_ref,
                      q_ref, k_ref, v_ref, g_ref, h_ref):
    tm = x_ref.shape[0]
    half = tm // 2
    grid_rows = half // GRID_W
    qw, kw = qw_ref[...], kw_ref[...]
    pair = 2 * HEAD_DIM
    g0 = ATTN_W + 2 * KV_W
    ones = jnp.ones((half, HEAD_DIM), _BF16)

    def rope_table(which, hf):
        return jnp.concatenate(
            [rrow_ref[which, hf * grid_rows + g:hf * grid_rows + g + 1, :] + rcol_ref[which]
             for g in range(grid_rows)], axis=0)

    for hf in range(2):
        rows = slice(hf * half, (hf + 1) * half)
        h_ref[rows, :] = (_rms(x_ref[rows, :]) * nw_ref[...]).astype(_BF16)
        cos, s_lo, s_hi = rope_table(0, hf), rope_table(1, hf), rope_table(2, hf)

        def norm_rope(t, w):
            tn = _rms(t) * w
            return (tn * cos + pltpu.roll(tn, HEAD_DIM - AXIS_ROT // 2, 1) * s_lo
                    + pltpu.roll(tn, AXIS_ROT // 2, 1) * s_hi)

        def proj(lo, width):
            return jnp.dot(h_ref[rows, :], w_ref[:, lo:lo + width], preferred_element_type=_F32)

        for c in range(ATTN_W // pair):
            acc = proj(c * pair, pair)
            for hh in range(2):
                t = norm_rope(acc[:, hh * HEAD_DIM:(hh + 1) * HEAD_DIM], qw) * QK_SCALE_LOG2
                q_ref[2 * c + hh, rows, :] = t.astype(_BF16)
        acc = proj(ATTN_W, KV_W)
        for hh in range(N_KV_HEADS):
            t = norm_rope(acc[:, hh * HEAD_DIM:(hh + 1) * HEAD_DIM], kw)
            k_ref[rows, hh * HEAD_DIM:(hh + 1) * HEAD_DIM] = t.astype(_BF16)
        acc = proj(ATTN_W + KV_W, KV_W)
        for hh in range(N_KV_HEADS):
            v_ref[rows, 2 * hh * HEAD_DIM:(2 * hh + 1) * HEAD_DIM] = (
                acc[:, hh * HEAD_DIM:(hh + 1) * HEAD_DIM].astype(_BF16))
            v_ref[rows, (2 * hh + 1) * HEAD_DIM:(2 * hh + 2) * HEAD_DIM] = ones
        for c in range(ATTN_W // 512):
            g_ref[rows, c * 512:(c + 1) * 512] = _silu(proj(g0 + c * 512, 512)).astype(_BF16)


def _attn_proj(xf, nw, w_in, layer, qw, kw, rope_row, rope_col, seq_len):
    n = xf.shape[0]
    tm = PROJ_TM
    assert tm // GRID_W == SUBLANES
    tiles_per_seq = seq_len // tm
    row = lambda i: (i, 0)
    const = lambda i: (0, 0)
    return pl.pallas_call(
        _attn_proj_kernel,
        grid=(n // tm,),
        in_specs=[
            pl.BlockSpec((tm, D_MODEL), row),
            pl.BlockSpec((1, D_MODEL), const),
            pl.BlockSpec((None, D_MODEL, ATTN_PROJ_W), lambda i: (layer, 0, 0), pipeline_mode=pl.Buffered(1)),
            pl.BlockSpec((1, HEAD_DIM), const),
            pl.BlockSpec((1, HEAD_DIM), const),
            pl.BlockSpec((3, SUBLANES, HEAD_DIM), lambda i: (0, i % tiles_per_seq, 0)),
            pl.BlockSpec((3, GRID_W, HEAD_DIM), lambda i: (0, 0, 0)),
        ],
        out_specs=[
            pl.BlockSpec((N_Q_HEADS, tm, HEAD_DIM), lambda i: (0, i, 0)),
            pl.BlockSpec((tm, KV_W), row),
            pl.BlockSpec((tm, 2 * KV_W), row),
            pl.BlockSpec((tm, ATTN_W), row),
            pl.BlockSpec((tm, D_MODEL), row),
        ],
        out_shape=[
            jax.ShapeDtypeStruct((N_Q_HEADS, n, HEAD_DIM), _BF16),
            jax.ShapeDtypeStruct((n, KV_W), _BF16),
            jax.ShapeDtypeStruct((n, 2 * KV_W), _BF16),
            jax.ShapeDtypeStruct((n, ATTN_W), _BF16),
            jax.ShapeDtypeStruct((n, D_MODEL), _BF16),
        ],
        compiler_params=pltpu.CompilerParams(
            dimension_semantics=("parallel",), vmem_limit_bytes=VMEM_LIMIT_BYTES),
        name="attn_proj",
    )(xf, nw, w_in, qw, kw, rope_row, rope_col)


def _rest_proj_kernel(h_ref, *refs):
    w_refs = refs[:REST_BLOCKS]
    snw_ref, sw_ref, sb_ref, hc_ref, cb_ref, gc_ref, os_ref, gs_ref = refs[REST_BLOCKS:]
    tm = h_ref.shape[0]

    def proj(idx):
        return jnp.dot(h_ref[...], w_refs[idx][...], preferred_element_type=_F32)

    u = _gelu(proj(4))
    v = _gelu(proj(5))
    hc_ref[...] = proj(2) * proj(0)
    n_chunks = tm // CHUNK
    for g in range(SGU_GROUPS):
        lo = g * SGU_GC
        vn = (_rms(v[:, lo:lo + SGU_GC]) * snw_ref[:, lo:lo + SGU_GC]).astype(_BF16)
        rhs = jnp.concatenate([vn[c * CHUNK:(c + 1) * CHUNK, :] for c in range(n_chunks)], axis=1)
        s = jnp.dot(sw_ref[g], rhs, preferred_element_type=_F32) + sb_ref[:, g:g + 1]
        for c in range(n_chunks):
            os_ref[c * CHUNK:(c + 1) * CHUNK, lo:lo + SGU_GC] = (
                u[c * CHUNK:(c + 1) * CHUNK, lo:lo + SGU_GC] * s[:, c * SGU_GC:(c + 1) * SGU_GC]
            ).astype(_BF16)
    gc_ref[...] = _silu(proj(3)).astype(_BF16)
    gs_ref[...] = _silu(proj(6)).astype(_BF16)
    cb_ref[...] = proj(1).astype(_BF16)


def _rest_proj(h, w_in, layer, snw, sw, sb_t):
    n = h.shape[0]
    tm = PROJ_TM
    row = lambda i: (i, 0)
    const = lambda i: (0, 0)
    outs = [jax.ShapeDtypeStruct((n, REST_BLOCK_W), dt) for dt in (_F32, _BF16, _BF16, _BF16, _BF16)]
    first_block = ATTN_PROJ_W // REST_BLOCK_W
    w_specs = [
        pl.BlockSpec((None, D_MODEL, REST_BLOCK_W), (lambda i, j=j: (layer, 0, first_block + j)),
                     pipeline_mode=pl.Buffered(1))
        for j in range(REST_BLOCKS)
    ]
    return pl.pallas_call(
        _rest_proj_kernel,
        grid=(n // tm,),
        in_specs=[
            pl.BlockSpec((tm, D_MODEL), row),
            *w_specs,
            pl.BlockSpec((1, SGU_W), const),
            pl.BlockSpec((SGU_GROUPS, CHUNK, CHUNK), lambda i: (0, 0, 0)),
            pl.BlockSpec((CHUNK, SGU_GROUPS), const),
        ],
        out_specs=[pl.BlockSpec((tm, REST_BLOCK_W), row)] * 5,
        out_shape=outs,
        compiler_params=pltpu.CompilerParams(
            dimension_semantics=("parallel",), vmem_limit_bytes=VMEM_LIMIT_BYTES),
        name="rest_proj",
    )(h, *([w_in] * REST_BLOCKS), snw, sw, sb_t)


def _flash_online_kernel(q_ref, k_ref, v_ref, o_ref, s_scr, p_scr, m_scr, l_scr, acc_scr, *, tk, rb):
    tq = q_ref.shape[1]
    n_chunks = k_ref.shape[0] // tk

    m_scr[...] = jnp.full(m_scr.shape, -jnp.inf, _F32)
    l_scr[...] = jnp.zeros(l_scr.shape, _F32)
    acc_scr[...] = jnp.zeros(acc_scr.shape, _F32)
    p_scr[1] = jnp.zeros(p_scr.shape[1:], _BF16)

    def kv_rows(c):
        return pl.ds(pl.multiple_of(c * tk, tk), tk)

    def qk(g, c):
        return lax.dot_general(q_ref[g], k_ref[kv_rows(c), :], (((1,), (1,)), ((), ())),
                               preferred_element_type=_F32)

    def pv(g, slot, c):
        acc_scr[g] += jnp.dot(p_scr[slot], v_ref[kv_rows(c), :], preferred_element_type=_F32)

    def softmax(g, slot):
        for r in range(tq // rb):
            rows = slice(r * rb, (r + 1) * rb)
            s = s_scr[slot, rows, :]
            m_prev = m_scr[g, rows, :]
            m_next = jnp.maximum(m_prev, jnp.max(s, axis=1, keepdims=True))
            alpha = jnp.exp2(m_prev - m_next)
            p = jnp.exp2(s - jnp.tile(m_next, (1, tk // LANES)))
            l_scr[g, rows, :] = alpha * l_scr[g, rows, :] + jnp.sum(p, axis=1, keepdims=True)
            m_scr[g, rows, :] = m_next
            acc_scr[g, rows, :] = alpha * acc_scr[g, rows, :]
            p_scr[slot, rows, :] = p.astype(_BF16)

    s_scr[0] = qk(0, 0)

    def body(c, carry):
        for g in range(Q_PER_KV):
            slot = g % 2
            if g < Q_PER_KV - 1:
                s_scr[1 - slot] = qk(g + 1, c)
            else:
                s_scr[1 - slot] = qk(0, jnp.minimum(c + 1, n_chunks - 1))
            softmax(g, slot)
            if g == 0:
                pv(Q_PER_KV - 1, 1, jnp.maximum(c - 1, 0))
            else:
                pv(g - 1, 1 - slot, c)
        return carry

    lax.fori_loop(0, n_chunks, body, 0)
    pv(Q_PER_KV - 1, 1, n_chunks - 1)

    for g in range(Q_PER_KV):
        o_ref[:, g * HEAD_DIM:(g + 1) * HEAD_DIM] = (acc_scr[g] / l_scr[g]).astype(_BF16)


def _flash_online(q, k, v1, batch, seq_len):
    tq, tk = ONLINE_TQ, ONLINE_TK
    q_tiles = seq_len // tq
    group_w = Q_PER_KV * HEAD_DIM
    kern = functools.partial(_flash_online_kernel, tk=tk, rb=SOFTMAX_ROWS)
    return pl.pallas_call(
        kern,
        grid=(batch, N_KV_HEADS, q_tiles),
        in_specs=[
            pl.BlockSpec((Q_PER_KV, tq, HEAD_DIM), lambda bi, hi, qi: (hi, bi * q_tiles + qi, 0)),
            pl.BlockSpec((seq_len, HEAD_DIM), lambda bi, hi, qi: (bi, hi)),
            pl.BlockSpec((seq_len, HEAD_DIM), lambda bi, hi, qi: (bi, 2 * hi)),
        ],
        out_specs=pl.BlockSpec((tq, group_w), lambda bi, hi, qi: (bi * q_tiles + qi, hi)),
        out_shape=jax.ShapeDtypeStruct((batch * seq_len, ATTN_W), _BF16),
        scratch_shapes=[
            pltpu.VMEM((2, tq, tk), _F32),
            pltpu.VMEM((2, tq, tk), _BF16),
            pltpu.VMEM((Q_PER_KV, tq, LANES), _F32),
            pltpu.VMEM((Q_PER_KV, tq, LANES), _F32),
            pltpu.VMEM((Q_PER_KV, tq, HEAD_DIM), _F32),
        ],
        compiler_params=pltpu.CompilerParams(
            dimension_semantics=("parallel", "parallel", "parallel"),
            vmem_limit_bytes=VMEM_LIMIT_BYTES),
        name="flash_online",
    )(q, k, v1)


def _flash_bounded_kernel(shift_ref, q_ref, k_ref, v_ref, o_ref, p_scr, acc_scr, *, tk):
    n_chunks = k_ref.shape[0] // tk
    shift = jnp.tile(shift_ref[...], (1, tk // LANES))

    def kv_rows(c):
        return pl.ds(pl.multiple_of(c * tk, tk), tk)

    def probs(g, c, slot):
        s = lax.dot_general(q_ref[g], k_ref[kv_rows(c), :], (((1,), (1,)), ((), ())),
                            preferred_element_type=_F32)
        p_scr[slot] = jnp.exp2(s - shift).astype(_BF16)

    def pv(g, slot, c, first):
        d = jnp.dot(p_scr[slot], v_ref[kv_rows(c), :], preferred_element_type=_F32)
        if first:
            acc_scr[g] = d
        else:
            acc_scr[g] += d

    def chunk(c, first=False, last=False):
        for g in range(Q_PER_KV):
            slot = g % 2
            if g < Q_PER_KV - 1:
                probs(g + 1, c, 1 - slot)
            elif not last:
                probs(0, c + 1, 1 - slot)
            pv(g, slot, c, first)

    probs(0, 0, 0)
    chunk(0, first=True)

    def body(c, carry):
        chunk(c)
        return carry

    lax.fori_loop(1, n_chunks - 1, body, 0)
    chunk(n_chunks - 1, last=True)

    for g in range(Q_PER_KV):
        a = acc_scr[g]
        o_ref[:, g * HEAD_DIM:(g + 1) * HEAD_DIM] = (a[:, :HEAD_DIM] / a[:, HEAD_DIM:]).astype(_BF16)


def _flash_bounded(q, k, v1, shift, batch, seq_len):
    tq, tk = BOUNDED_TQ, BOUNDED_TK
    assert seq_len // tk >= 2
    q_tiles = seq_len // tq
    group_w = Q_PER_KV * HEAD_DIM
    kern = functools.partial(_flash_bounded_kernel, tk=tk)
    return pl.pallas_call(
        kern,
        grid=(batch, N_KV_HEADS, q_tiles),
        in_specs=[
            pl.BlockSpec((1, LANES), lambda bi, hi, qi: (0, 0)),
            pl.BlockSpec((Q_PER_KV, tq, HEAD_DIM), lambda bi, hi, qi: (hi, bi * q_tiles + qi, 0)),
            pl.BlockSpec((seq_len, HEAD_DIM), lambda bi, hi, qi: (bi, hi)),
            pl.BlockSpec((seq_len, 2 * HEAD_DIM), lambda bi, hi, qi: (bi, hi)),
        ],
        out_specs=pl.BlockSpec((tq, group_w), lambda bi, hi, qi: (bi * q_tiles + qi, hi)),
        out_shape=jax.ShapeDtypeStruct((batch * seq_len, ATTN_W), _BF16),
        scratch_shapes=[
            pltpu.VMEM((2, tq, tk), _BF16),
            pltpu.VMEM((Q_PER_KV, tq, 2 * HEAD_DIM), _F32),
        ],
        compiler_params=pltpu.CompilerParams(
            dimension_semantics=("parallel", "parallel", "parallel"),
            vmem_limit_bytes=VMEM_LIMIT_BYTES),
        name="flash_bounded",
    )(shift, q, k, v1)


def _score_bound(q_norm_w, k_norm_w):
    rounding = (1.0 + 2.0 ** -8) ** 2
    return (HEAD_DIM * QK_SCALE_LOG2 * rounding) * jnp.max(jnp.abs(q_norm_w)) * jnp.max(jnp.abs(k_norm_w))


def _attention(q, k, v1, q_norm_w, k_norm_w, batch, seq_len):
    shift = _score_bound(q_norm_w, k_norm_w)
    return lax.cond(
        shift <= MAX_BOUNDED_SHIFT,
        lambda: _flash_bounded(q, k, v1, jnp.full((1, LANES), shift, _F32), batch, seq_len),
        lambda: _flash_online(q, k, v1, batch, seq_len))


def _mix_kernel(x_ref, oa_ref, ga_ref, hc_ref, hp_ref, hn_ref, cb_ref, gc_ref, os_ref, gs_ref,
                cw_ref, bw_ref, w_ref, fw_ref, out_ref, mixed_scr, *, tiles_per_seq, final_norm):
    i = pl.program_id(0)
    tm = x_ref.shape[0]
    f32 = lambda ref: ref[...].astype(_F32)
    mixed_scr[:, :ATTN_W] = (_rms(f32(oa_ref)) * bw_ref[:, :ATTN_W] * f32(ga_ref)).astype(_BF16)

    hc = hc_ref[...]
    t = i % tiles_per_seq
    prev_row = jnp.where(t == 0, 0.0, hp_ref[SUBLANES - 1:SUBLANES, :])
    next_row = jnp.where(t == tiles_per_seq - 1, 0.0, hn_ref[0:1, :])
    ridx = lax.broadcasted_iota(jnp.int32, hc.shape, 0)
    up = jnp.where(ridx == 0, prev_row, pltpu.roll(hc, 1, 0))
    down = jnp.where(ridx == tm - 1, next_row, pltpu.roll(hc, tm - 1, 0))
    y = f32(cb_ref) * (up * cw_ref[0:1, :] + hc * cw_ref[1:2, :] + down * cw_ref[2:3, :])
    c0 = ATTN_W
    mixed_scr[:, c0:c0 + CONV_W] = (_rms(y) * bw_ref[:, c0:c0 + CONV_W] * f32(gc_ref)).astype(_BF16)
    c1 = ATTN_W + CONV_W
    mixed_scr[:, c1:] = (_rms(f32(os_ref)) * bw_ref[:, c1:] * f32(gs_ref)).astype(_BF16)

    out = x_ref[...] + jnp.dot(mixed_scr[...], w_ref[...], preferred_element_type=_F32)
    if final_norm:
        out = _rms(out) * fw_ref[...]
    out_ref[...] = out


def _mix(xf, oa, ga, hc, cb, gc, osg, gs, cw_t, bw, w_out, layer, fw, seq_len, final_norm):
    n = xf.shape[0]
    tm = MIX_TM
    tiles_per_seq = seq_len // tm
    halo_blocks = tm // SUBLANES
    n_halo = n // SUBLANES
    row = lambda i: (i, 0)
    const = lambda i: (0, 0)
    prev = lambda i: (jnp.maximum(i * halo_blocks - 1, 0), 0)
    nxt = lambda i: (jnp.minimum((i + 1) * halo_blocks, n_halo - 1), 0)
    kern = functools.partial(_mix_kernel, tiles_per_seq=tiles_per_seq, final_norm=final_norm)
    return pl.pallas_call(
        kern,
        grid=(n // tm,),
        in_specs=[
            pl.BlockSpec((tm, D_MODEL), row),
            pl.BlockSpec((tm, ATTN_W), row),
            pl.BlockSpec((tm, ATTN_W), row),
            pl.BlockSpec((tm, CONV_W), row),
            pl.BlockSpec((SUBLANES, CONV_W), prev),
            pl.BlockSpec((SUBLANES, CONV_W), nxt),
            pl.BlockSpec((tm, CONV_W), row),
            pl.BlockSpec((tm, CONV_W), row),
            pl.BlockSpec((tm, SGU_W), row),
            pl.BlockSpec((tm, SGU_W), row),
            pl.BlockSpec((3, CONV_W), const),
            pl.BlockSpec((1, MIX_W), const),
            pl.BlockSpec((None, MIX_W, D_MODEL), lambda i: (layer, 0, 0), pipeline_mode=pl.Buffered(1)),
            pl.BlockSpec((1, D_MODEL), const),
        ],
        out_specs=pl.BlockSpec((tm, D_MODEL), row),
        out_shape=jax.ShapeDtypeStruct((n, D_MODEL), _F32),
        scratch_shapes=[pltpu.VMEM((tm, MIX_W), _BF16)],
        compiler_params=pltpu.CompilerParams(
            dimension_semantics=("parallel",), vmem_limit_bytes=VMEM_LIMIT_BYTES),
        name="mix_out",
    )(xf, oa, ga, hc, hc, hc, cb, gc, osg, gs, cw_t, bw, w_out, fw)


def kernel(x, norm_w, w_in, q_norm_w, k_norm_w, conv_w, sgu_norm_w, sgu_w, sgu_b, branch_norm_w, w_out,
           final_norm_w):
    b, s, d = x.shape
    depth = w_in.shape[0]
    n = b * s
    rope_row, rope_col = _rope_tables(s)
    sw = sgu_w.astype(_BF16)
    xf = x.reshape(n, d)
    for l in range(depth):
        q, k, v1, ga, h = _attn_proj(xf, norm_w[l].reshape(1, d), w_in, l, q_norm_w[l].reshape(1, HEAD_DIM),
                                     k_norm_w[l].reshape(1, HEAD_DIM), rope_row, rope_col, s)
        hc, cb, gc, osg, gs = _rest_proj(h, w_in, l, sgu_norm_w[l].reshape(1, SGU_W), sw[l], sgu_b[l].T)
        oa = _attention(q, k, v1, q_norm_w[l], k_norm_w[l], b, s)
        xf = _mix(xf, oa, ga, hc, cb, gc, osg, gs, conv_w[l].T, branch_norm_w[l].reshape(1, MIX_W),
                  w_out, l, final_norm_w.reshape(1, d), s, final_norm=(l == depth - 1))
    return xf.reshape(b, s, d)
```

```python
import functools

import jax
import jax.numpy as jnp
import numpy as np
from jax import lax
from jax.experimental import pallas as pl
from jax.experimental.pallas import tpu as pltpu

D_MODEL = 2048
HEAD_DIM = 128
N_Q_HEADS = 8
N_KV_HEADS = 2
Q_PER_KV = N_Q_HEADS // N_KV_HEADS
ATTN_W = N_Q_HEADS * HEAD_DIM
KV_W = N_KV_HEADS * HEAD_DIM
CONV_W = 512
SGU_GROUPS = 4
SGU_GC = 128
SGU_W = SGU_GROUPS * SGU_GC
CHUNK = 128
GRID_W = 64
ROPE_THETA = 10000.0
AXIS_ROT = HEAD_DIM // 2
EPS = 1e-6
MIX_W = ATTN_W + CONV_W + SGU_W
ATTN_PROJ_W = 2 * ATTN_W + 2 * KV_W
REST_BLOCK_W = 512
REST_BLOCKS = 7
assert CONV_W == SGU_W == REST_BLOCK_W and ATTN_PROJ_W % REST_BLOCK_W == 0
QK_SCALE_LOG2 = float(HEAD_DIM ** -0.5 * np.log2(np.e))

LANES = 128
SUBLANES = 8
VMEM_LIMIT_BYTES = 56 * 1024 * 1024

PROJ_TM = 512
MIX_TM = 512
BOUNDED_TQ = 1024
ONLINE_TQ = 512
BOUNDED_TK = 2048
ONLINE_TK = 512
SOFTMAX_ROWS = 64
assert Q_PER_KV % 2 == 0

MAX_BOUNDED_SHIFT = 48.0

_BF16 = jnp.bfloat16
_F32 = jnp.float32


def _rms(t):
    return t * lax.rsqrt(jnp.mean(t * t, axis=-1, keepdims=True) + EPS)


def _silu(t):
    return t * (1.0 / (1.0 + jnp.exp(-t)))


def _gelu(t):
    return 0.5 * t * (1.0 + lax.erf(t * np.float32(np.sqrt(0.5))))


def _rope_tables(seq_len):
    inv_freq = ROPE_THETA ** (-jnp.arange(0, AXIS_ROT, 2, dtype=_F32) / AXIS_ROT)
    first_half = jnp.arange(AXIS_ROT) < (AXIS_ROT // 2)

    def axis_table(count):
        ang = jnp.arange(count, dtype=_F32)[:, None] * inv_freq
        ang = jnp.concatenate([ang, ang], axis=1)
        cos, sin = jnp.cos(ang), jnp.sin(ang)
        return jnp.stack([cos, jnp.where(first_half, -sin, 0.0), jnp.where(first_half, 0.0, sin)])

    row_t, col_t = axis_table(seq_len // GRID_W), axis_table(GRID_W)
    rope_row = jnp.concatenate([row_t, jnp.zeros_like(row_t)], axis=-1)
    rope_col = jnp.concatenate([jnp.zeros_like(col_t), col_t], axis=-1)
    return rope_row, rope_col


def _attn_proj_kernel(x_ref, nw_ref, w_ref, qw_ref, kw_ref, rrow_ref, rcol_ref,
                      q_ref, k_ref, v_ref, vt_ref, g_ref, h_ref):
    tm = x_ref.shape[0]
    half = tm // 2
    grid_rows = half // GRID_W
    qw, kw = qw_ref[...], kw_ref[...]
    pair = 2 * HEAD_DIM
    g0 = ATTN_W + 2 * KV_W

    def rope_table(which, hf):
        return jnp.concatenate(
            [rrow_ref[which, hf * grid_rows + g:hf * grid_rows + g + 1, :] + rcol_ref[which]
             for g in range(grid_rows)], axis=0)

    for hf in range(2):
        rows = slice(hf * half, (hf + 1) * half)
        h_ref[rows, :] = (_rms(x_ref[rows, :]) * nw_ref[...]).astype(_BF16)
        cos, s_lo, s_hi = rope_table(0, hf), rope_table(1, hf), rope_table(2, hf)

        def norm_rope(t, w):
            tn = _rms(t) * w
            return (tn * cos + pltpu.roll(tn, HEAD_DIM - AXIS_ROT // 2, 1) * s_lo
                    + pltpu.roll(tn, AXIS_ROT // 2, 1) * s_hi)

        def proj(lo, width):
            return jnp.dot(h_ref[rows, :], w_ref[:, lo:lo + width], preferred_element_type=_F32)

        for c in range(ATTN_W // pair):
            acc = proj(c * pair, pair)
            for hh in range(2):
                t = norm_rope(acc[:, hh * HEAD_DIM:(hh + 1) * HEAD_DIM], qw) * QK_SCALE_LOG2
                q_ref[2 * c + hh, rows, :] = t.astype(_BF16)
        acc = proj(ATTN_W, KV_W)
        for hh in range(N_KV_HEADS):
            t = norm_rope(acc[:, hh * HEAD_DIM:(hh + 1) * HEAD_DIM], kw)
            k_ref[rows, hh * HEAD_DIM:(hh + 1) * HEAD_DIM] = t.astype(_BF16)
        acc = proj(ATTN_W + KV_W, KV_W)
        v_ref[rows, :] = acc.astype(_BF16)
        vt_ref[:, rows] = acc.T.astype(_BF16)
        for c in range(ATTN_W // 512):
            g_ref[rows, c * 512:(c + 1) * 512] = _silu(proj(g0 + c * 512, 512)).astype(_BF16)


def _attn_proj(xf, nw, w_in, layer, qw, kw, rope_row, rope_col, seq_len):
    n = xf.shape[0]
    tm = PROJ_TM
    assert tm // GRID_W == SUBLANES
    tiles_per_seq = seq_len // tm
    row = lambda i: (i, 0)
    const = lambda i: (0, 0)
    return pl.pallas_call(
        _attn_proj_kernel,
        grid=(n // tm,),
        in_specs=[
            pl.BlockSpec((tm, D_MODEL), row),
            pl.BlockSpec((1, D_MODEL), const),
            pl.BlockSpec((None, D_MODEL, ATTN_PROJ_W), lambda i: (layer, 0, 0), pipeline_mode=pl.Buffered(1)),
            pl.BlockSpec((1, HEAD_DIM), const),
            pl.BlockSpec((1, HEAD_DIM), const),
            pl.BlockSpec((3, SUBLANES, HEAD_DIM), lambda i: (0, i % tiles_per_seq, 0)),
            pl.BlockSpec((3, GRID_W, HEAD_DIM), lambda i: (0, 0, 0)),
        ],
        out_specs=[
            pl.BlockSpec((N_Q_HEADS, tm, HEAD_DIM), lambda i: (0, i, 0)),
            pl.BlockSpec((tm, KV_W), row),
            pl.BlockSpec((tm, KV_W), row),
            pl.BlockSpec((KV_W, tm), lambda i: (0, i)),
            pl.BlockSpec((tm, ATTN_W), row),
            pl.BlockSpec((tm, D_MODEL), row),
        ],
        out_shape=[
            jax.ShapeDtypeStruct((N_Q_HEADS, n, HEAD_DIM), _BF16),
            jax.ShapeDtypeStruct((n, KV_W), _BF16),
            jax.ShapeDtypeStruct((n, KV_W), _BF16),
            jax.ShapeDtypeStruct((KV_W, n), _BF16),
            jax.ShapeDtypeStruct((n, ATTN_W), _BF16),
            jax.ShapeDtypeStruct((n, D_MODEL), _BF16),
        ],
        compiler_params=pltpu.CompilerParams(
            dimension_semantics=("parallel",), vmem_limit_bytes=VMEM_LIMIT_BYTES),
        name="attn_proj",
    )(xf, nw, w_in, qw, kw, rope_row, rope_col)


def _rest_proj_kernel(h_ref, *refs):
    w_refs = refs[:REST_BLOCKS]
    snw_ref, sw_ref, sb_ref, hc_ref, cb_ref, gc_ref, os_ref, gs_ref = refs[REST_BLOCKS:]
    tm = h_ref.shape[0]

    def proj(idx):
        return jnp.dot(h_ref[...], w_refs[idx][...], preferred_element_type=_F32)

    u = _gelu(proj(4))
    v = _gelu(proj(5))
    hc_ref[...] = proj(2) * proj(0)
    n_chunks = tm // CHUNK
    for g in range(SGU_GROUPS):
        lo = g * SGU_GC
        vn = (_rms(v[:, lo:lo + SGU_GC]) * snw_ref[:, lo:lo + SGU_GC]).astype(_BF16)
        rhs = jnp.concatenate([vn[c * CHUNK:(c + 1) * CHUNK, :] for c in range(n_chunks)], axis=1)
        s = jnp.dot(sw_ref[g], rhs, preferred_element_type=_F32) + sb_ref[:, g:g + 1]
        for c in range(n_chunks):
            os_ref[c * CHUNK:(c + 1) * CHUNK, lo:lo + SGU_GC] = (
                u[c * CHUNK:(c + 1) * CHUNK, lo:lo + SGU_GC] * s[:, c * SGU_GC:(c + 1) * SGU_GC]
            ).astype(_BF16)
    gc_ref[...] = _silu(proj(3)).astype(_BF16)
    gs_ref[...] = _silu(proj(6)).astype(_BF16)
    cb_ref[...] = proj(1).astype(_BF16)


def _rest_proj(h, w_in, layer, snw, sw, sb_t):
    n = h.shape[0]
    tm = PROJ_TM
    row = lambda i: (i, 0)
    const = lambda i: (0, 0)
    outs = [jax.ShapeDtypeStruct((n, REST_BLOCK_W), dt) for dt in (_F32, _BF16, _BF16, _BF16, _BF16)]
    first_block = ATTN_PROJ_W // REST_BLOCK_W
    w_specs = [
        pl.BlockSpec((None, D_MODEL, REST_BLOCK_W), (lambda i, j=j: (layer, 0, first_block + j)),
                     pipeline_mode=pl.Buffered(1))
        for j in range(REST_BLOCKS)
    ]
    return pl.pallas_call(
        _rest_proj_kernel,
        grid=(n // tm,),
        in_specs=[
            pl.BlockSpec((tm, D_MODEL), row),
            *w_specs,
            pl.BlockSpec((1, SGU_W), const),
            pl.BlockSpec((SGU_GROUPS, CHUNK, CHUNK), lambda i: (0, 0, 0)),
            pl.BlockSpec((CHUNK, SGU_GROUPS), const),
        ],
        out_specs=[pl.BlockSpec((tm, REST_BLOCK_W), row)] * 5,
        out_shape=outs,
        compiler_params=pltpu.CompilerParams(
            dimension_semantics=("parallel",), vmem_limit_bytes=VMEM_LIMIT_BYTES),
        name="rest_proj",
    )(h, *([w_in] * REST_BLOCKS), snw, sw, sb_t)


def _flash_online_kernel(q_ref, k_ref, v_ref, o_ref, s_scr, p_scr, m_scr, l_scr, acc_scr, *, tk, rb):
    tq = q_ref.shape[1]
    n_chunks = k_ref.shape[0] // tk

    m_scr[...] = jnp.full(m_scr.shape, -jnp.inf, _F32)
    l_scr[...] = jnp.zeros(l_scr.shape, _F32)
    acc_scr[...] = jnp.zeros(acc_scr.shape, _F32)
    p_scr[1] = jnp.zeros(p_scr.shape[1:], _BF16)

    def kv_rows(c):
        return pl.ds(pl.multiple_of(c * tk, tk), tk)

    def qk(g, c):
        return lax.dot_general(q_ref[g], k_ref[kv_rows(c), :], (((1,), (1,)), ((), ())),
                               preferred_element_type=_F32)

    def pv(g, slot, c):
        acc_scr[g] += jnp.dot(p_scr[slot], v_ref[kv_rows(c), :], preferred_element_type=_F32)

    def softmax(g, slot):
        for r in range(tq // rb):
            rows = slice(r * rb, (r + 1) * rb)
            s = s_scr[slot, rows, :]
            m_prev = m_scr[g, rows, :]
            m_next = jnp.maximum(m_prev, jnp.max(s, axis=1, keepdims=True))
            alpha = jnp.exp2(m_prev - m_next)
            p = jnp.exp2(s - jnp.tile(m_next, (1, tk // LANES)))
            l_scr[g, rows, :] = alpha * l_scr[g, rows, :] + jnp.sum(p, axis=1, keepdims=True)
            m_scr[g, rows, :] = m_next
            acc_scr[g, rows, :] = alpha * acc_scr[g, rows, :]
            p_scr[slot, rows, :] = p.astype(_BF16)

    s_scr[0] = qk(0, 0)

    def body(c, carry):
        for g in range(Q_PER_KV):
            slot = g % 2
            if g < Q_PER_KV - 1:
                s_scr[1 - slot] = qk(g + 1, c)
            else:
                s_scr[1 - slot] = qk(0, jnp.minimum(c + 1, n_chunks - 1))
            softmax(g, slot)
            if g == 0:
                pv(Q_PER_KV - 1, 1, jnp.maximum(c - 1, 0))
            else:
                pv(g - 1, 1 - slot, c)
        return carry

    lax.fori_loop(0, n_chunks, body, 0)
    pv(Q_PER_KV - 1, 1, n_chunks - 1)

    for g in range(Q_PER_KV):
        o_ref[:, g * HEAD_DIM:(g + 1) * HEAD_DIM] = (acc_scr[g] / l_scr[g]).astype(_BF16)


def _flash_online(q, k, v, batch, seq_len):
    tq, tk = ONLINE_TQ, ONLINE_TK
    q_tiles = seq_len // tq
    group_w = Q_PER_KV * HEAD_DIM
    kern = functools.partial(_flash_online_kernel, tk=tk, rb=SOFTMAX_ROWS)
    return pl.pallas_call(
        kern,
        grid=(batch, N_KV_HEADS, q_tiles),
        in_specs=[
            pl.BlockSpec((Q_PER_KV, tq, HEAD_DIM), lambda bi, hi, qi: (hi, bi * q_tiles + qi, 0)),
            pl.BlockSpec((seq_len, HEAD_DIM), lambda bi, hi, qi: (bi, hi)),
            pl.BlockSpec((seq_len, HEAD_DIM), lambda bi, hi, qi: (bi, hi)),
        ],
        out_specs=pl.BlockSpec((tq, group_w), lambda bi, hi, qi: (bi * q_tiles + qi, hi)),
        out_shape=jax.ShapeDtypeStruct((batch * seq_len, ATTN_W), _BF16),
        scratch_shapes=[
            pltpu.VMEM((2, tq, tk), _F32),
            pltpu.VMEM((2, tq, tk), _BF16),
            pltpu.VMEM((Q_PER_KV, tq, LANES), _F32),
            pltpu.VMEM((Q_PER_KV, tq, LANES), _F32),
            pltpu.VMEM((Q_PER_KV, tq, HEAD_DIM), _F32),
        ],
        compiler_params=pltpu.CompilerParams(
            dimension_semantics=("parallel", "parallel", "parallel"),
            vmem_limit_bytes=VMEM_LIMIT_BYTES),
        name="flash_online",
    )(q, k, v)

def _flash_bounded_kernel(shift_ref, q_ref, k_ref, vt_ref, o_ref, p_scr, l_scr, acc_scr, *, tk):
    tq = q_ref.shape[1]
    n_chunks = k_ref.shape[0] // tk
    shift = shift_ref[...]

    def kv_rows(c):
        return pl.ds(pl.multiple_of(c * tk, tk), tk)

    def probs(g, c, slot, first):
        s = lax.dot_general(k_ref[kv_rows(c), :], q_ref[g], (((1,), (1,)), ((), ())),
                            preferred_element_type=_F32)
        p = jnp.exp2(s - shift)
        part = jnp.sum(p.reshape(tk // SUBLANES, SUBLANES, tq), axis=0)
        if first:
            l_scr[g] = part
        else:
            l_scr[g] += part
        p_scr[slot] = p.astype(_BF16)

    def pv(g, slot, c, first):
        d = jnp.dot(vt_ref[:, kv_rows(c)], p_scr[slot], preferred_element_type=_F32)
        if first:
            acc_scr[g] = d
        else:
            acc_scr[g] += d

    def chunk(c, first=False, last=False):
        for g in range(Q_PER_KV):
            slot = g % 2
            if g < Q_PER_KV - 1:
                probs(g + 1, c, 1 - slot, first)
            elif not last:
                probs(0, c + 1, 1 - slot, False)
            pv(g, slot, c, first)

    probs(0, 0, 0, True)
    chunk(0, first=True)

    def body(c, carry):
        chunk(c)
        return carry

    lax.fori_loop(1, n_chunks - 1, body, 0)
    chunk(n_chunks - 1, last=True)

    for g in range(Q_PER_KV):
        l = jnp.sum(l_scr[g], axis=0, keepdims=True)
        o_ref[:, g * HEAD_DIM:(g + 1) * HEAD_DIM] = (acc_scr[g] / l).T.astype(_BF16)


def _flash_bounded(q, k, vt, shift, batch, seq_len):
    tq, tk = BOUNDED_TQ, BOUNDED_TK
    assert seq_len // tk >= 2
    q_tiles = seq_len // tq
    group_w = Q_PER_KV * HEAD_DIM
    kern = functools.partial(_flash_bounded_kernel, tk=tk)
    return pl.pallas_call(
        kern,
        grid=(batch, N_KV_HEADS, q_tiles),
        in_specs=[
            pl.BlockSpec((1, tq), lambda bi, hi, qi: (0, 0)),
            pl.BlockSpec((Q_PER_KV, tq, HEAD_DIM), lambda bi, hi, qi: (hi, bi * q_tiles + qi, 0)),
            pl.BlockSpec((seq_len, HEAD_DIM), lambda bi, hi, qi: (bi, hi)),
            pl.BlockSpec((HEAD_DIM, seq_len), lambda bi, hi, qi: (hi, bi)),
        ],
        out_specs=pl.BlockSpec((tq, group_w), lambda bi, hi, qi: (bi * q_tiles + qi, hi)),
        out_shape=jax.ShapeDtypeStruct((batch * seq_len, ATTN_W), _BF16),
        scratch_shapes=[
            pltpu.VMEM((2, tk, tq), _BF16),
            pltpu.VMEM((Q_PER_KV, SUBLANES, tq), _F32),
            pltpu.VMEM((Q_PER_KV, HEAD_DIM, tq), _F32),
        ],
        compiler_params=pltpu.CompilerParams(
            dimension_semantics=("parallel", "parallel", "parallel"),
            vmem_limit_bytes=VMEM_LIMIT_BYTES),
        name="flash_bounded",
    )(shift, q, k, vt)


def _score_bound(q_norm_w, k_norm_w):
    rounding = (1.0 + 2.0 ** -8) ** 2
    return (HEAD_DIM * QK_SCALE_LOG2 * rounding) * jnp.max(jnp.abs(q_norm_w)) * jnp.max(jnp.abs(k_norm_w))


def _attention(q, k, v, vt, q_norm_w, k_norm_w, batch, seq_len):
    shift = _score_bound(q_norm_w, k_norm_w)
    return lax.cond(
        shift <= MAX_BOUNDED_SHIFT,
        lambda: _flash_bounded(q, k, vt, jnp.full((1, BOUNDED_TQ), shift, _F32), batch, seq_len),
        lambda: _flash_online(q, k, v, batch, seq_len))


def _mix_kernel(x_ref, oa_ref, ga_ref, hc_ref, hp_ref, hn_ref, cb_ref, gc_ref, os_ref, gs_ref,
                cw_ref, bw_ref, w_ref, fw_ref, out_ref, mixed_scr, *, tiles_per_seq, final_norm):
    i = pl.program_id(0)
    tm = x_ref.shape[0]
    f32 = lambda ref: ref[...].astype(_F32)
    mixed_scr[:, :ATTN_W] = (_rms(f32(oa_ref)) * bw_ref[:, :ATTN_W] * f32(ga_ref)).astype(_BF16)

    hc = hc_ref[...]
    t = i % tiles_per_seq
    prev_row = jnp.where(t == 0, 0.0, hp_ref[SUBLANES - 1:SUBLANES, :])
    next_row = jnp.where(t == tiles_per_seq - 1, 0.0, hn_ref[0:1, :])
    ridx = lax.broadcasted_iota(jnp.int32, hc.shape, 0)
    up = jnp.where(ridx == 0, prev_row, pltpu.roll(hc, 1, 0))
    down = jnp.where(ridx == tm - 1, next_row, pltpu.roll(hc, tm - 1, 0))
    y = f32(cb_ref) * (up * cw_ref[0:1, :] + hc * cw_ref[1:2, :] + down * cw_ref[2:3, :])
    c0 = ATTN_W
    mixed_scr[:, c0:c0 + CONV_W] = (_rms(y) * bw_ref[:, c0:c0 + CONV_W] * f32(gc_ref)).astype(_BF16)
    c1 = ATTN_W + CONV_W
    mixed_scr[:, c1:] = (_rms(f32(os_ref)) * bw_ref[:, c1:] * f32(gs_ref)).astype(_BF16)

    out = x_ref[...] + jnp.dot(mixed_scr[...], w_ref[...], preferred_element_type=_F32)
    if final_norm:
        out = _rms(out) * fw_ref[...]
    out_ref[...] = out


def _mix(xf, oa, ga, hc, cb, gc, osg, gs, cw_t, bw, w_out, layer, fw, seq_len, final_norm):
    n = xf.shape[0]
    tm = MIX_TM
    tiles_per_seq = seq_len // tm
    halo_blocks = tm // SUBLANES
    n_halo = n // SUBLANES
    row = lambda i: (i, 0)
    const = lambda i: (0, 0)
    prev = lambda i: (jnp.maximum(i * halo_blocks - 1, 0), 0)
    nxt = lambda i: (jnp.minimum((i + 1) * halo_blocks, n_halo - 1), 0)
    kern = functools.partial(_mix_kernel, tiles_per_seq=tiles_per_seq, final_norm=final_norm)
    return pl.pallas_call(
        kern,
        grid=(n // tm,),
        in_specs=[
            pl.BlockSpec((tm, D_MODEL), row),
            pl.BlockSpec((tm, ATTN_W), row),
            pl.BlockSpec((tm, ATTN_W), row),
            pl.BlockSpec((tm, CONV_W), row),
            pl.BlockSpec((SUBLANES, CONV_W), prev),
            pl.BlockSpec((SUBLANES, CONV_W), nxt),
            pl.BlockSpec((tm, CONV_W), row),
            pl.BlockSpec((tm, CONV_W), row),
            pl.BlockSpec((tm, SGU_W), row),
            pl.BlockSpec((tm, SGU_W), row),
            pl.BlockSpec((3, CONV_W), const),
            pl.BlockSpec((1, MIX_W), const),
            pl.BlockSpec((None, MIX_W, D_MODEL), lambda i: (layer, 0, 0), pipeline_mode=pl.Buffered(1)),
            pl.BlockSpec((1, D_MODEL), const),
        ],
        out_specs=pl.BlockSpec((tm, D_MODEL), row),
        out_shape=jax.ShapeDtypeStruct((n, D_MODEL), _F32),
        scratch_shapes=[pltpu.VMEM((tm, MIX_W), _BF16)],
        compiler_params=pltpu.CompilerParams(
            dimension_semantics=("parallel",), vmem_limit_bytes=VMEM_LIMIT_BYTES),
        name="mix_out",
    )(xf, oa, ga, hc, hc, hc, cb, gc, osg, gs, cw_t, bw, w_out, fw)


def kernel(x, norm_w, w_in, q_norm_w, k_norm_w, conv_w, sgu_norm_w, sgu_w, sgu_b, branch_norm_w, w_out,
           final_norm_w):
    b, s, d = x.shape
    depth = w_in.shape[0]
    n = b * s
    rope_row, rope_col = _rope_tables(s)
    sw = sgu_w.astype(_BF16)
    xf = x.reshape(n, d)
    for l in range(depth):
        q, k, v, vt, ga, h = _attn_proj(xf, norm_w[l].reshape(1, d), w_in, l, q_norm_w[l].reshape(1, HEAD_DIM),
                                     k_norm_w[l].reshape(1, HEAD_DIM), rope_row, rope_col, s)
        hc, cb, gc, osg, gs = _rest_proj(h, w_in, l, sgu_norm_w[l].reshape(1, SGU_W), sw[l], sgu_b[l].T)
        oa = _attention(q, k, v, vt, q_norm_w[l], k_norm_w[l], b, s)
        xf = _mix(xf, oa, ga, hc, cb, gc, osg, gs, conv_w[l].T, branch_norm_w[l].reshape(1, MIX_W),
                  w_out, l, final_norm_w.reshape(1, d), s, final_norm=(l == depth - 1))
    return xf.reshape(b, s, d)
```

```python
import functools

import jax
import jax.numpy as jnp
import numpy as np
from jax import lax
from jax.experimental import pallas as pl
from jax.experimental.pallas import tpu as pltpu

D_MODEL = 2048
HEAD_DIM = 128
N_Q_HEADS = 8
N_KV_HEADS = 2
Q_PER_KV = N_Q_HEADS // N_KV_HEADS
ATTN_W = N_Q_HEADS * HEAD_DIM
KV_W = N_KV_HEADS * HEAD_DIM
CONV_W = 512
SGU_GROUPS = 4
SGU_GC = 128
SGU_W = SGU_GROUPS * SGU_GC
CHUNK = 128
GRID_W = 64
ROPE_THETA = 10000.0
AXIS_ROT = HEAD_DIM // 2
EPS = 1e-6
MIX_W = ATTN_W + CONV_W + SGU_W
ATTN_PROJ_W = 2 * ATTN_W + 2 * KV_W
REST_BLOCK_W = 512
REST_BLOCKS = 7
assert CONV_W == SGU_W == REST_BLOCK_W and ATTN_PROJ_W % REST_BLOCK_W == 0
QK_SCALE_LOG2 = float(HEAD_DIM ** -0.5 * np.log2(np.e))

LANES = 128
SUBLANES = 8
VMEM_LIMIT_BYTES = 56 * 1024 * 1024

PROJ_TM = 512
MIX_TM = 512
BOUNDED_TQ = 1024
ONLINE_TQ = 512
BOUNDED_TK = 2048
ONLINE_TK = 512
SOFTMAX_ROWS = 64
assert Q_PER_KV % 2 == 0

MAX_BOUNDED_SHIFT = 48.0

_BF16 = jnp.bfloat16
_F32 = jnp.float32


def _rms(t):
    return t * lax.rsqrt(jnp.mean(t * t, axis=-1, keepdims=True) + EPS)


def _silu(t):
    return t * (1.0 / (1.0 + jnp.exp(-t)))


def _gelu(t):
    return 0.5 * t * (1.0 + lax.erf(t * np.float32(np.sqrt(0.5))))


def _rope_tables(seq_len):
    inv_freq = ROPE_THETA ** (-jnp.arange(0, AXIS_ROT, 2, dtype=_F32) / AXIS_ROT)
    first_half = jnp.arange(AXIS_ROT) < (AXIS_ROT // 2)

    def axis_table(count):
        ang = jnp.arange(count, dtype=_F32)[:, None] * inv_freq
        ang = jnp.concatenate([ang, ang], axis=1)
        cos, sin = jnp.cos(ang), jnp.sin(ang)
        return jnp.stack([cos, jnp.where(first_half, -sin, 0.0), jnp.where(first_half, 0.0, sin)])

    row_t, col_t = axis_table(seq_len // GRID_W), axis_table(GRID_W)
    rope_row = jnp.concatenate([row_t, jnp.zeros_like(row_t)], axis=-1)
    rope_col = jnp.concatenate([jnp.zeros_like(col_t), col_t], axis=-1)
    return rope_row, rope_col


def _attn_proj_kernel(x_ref, nw_ref, w_ref, qw_ref, kw_ref, rrow_ref, rcol_ref,
                      q_ref, k_ref, v_ref, vt_ref, g_ref, h_ref):
    tm = x_ref.shape[0]
    half = tm // 2
    grid_rows = half // GRID_W
    qw, kw = qw_ref[...], kw_ref[...]
    pair = 2 * HEAD_DIM
    g0 = ATTN_W + 2 * KV_W

    def rope_table(which, hf):
        return jnp.concatenate(
            [rrow_ref[which, hf * grid_rows + g:hf * grid_rows + g + 1, :] + rcol_ref[which]
             for g in range(grid_rows)], axis=0)

    for hf in range(2):
        rows = slice(hf * half, (hf + 1) * half)
        h_ref[rows, :] = (_rms(x_ref[rows, :]) * nw_ref[...]).astype(_BF16)
        cos, s_lo, s_hi = rope_table(0, hf), rope_table(1, hf), rope_table(2, hf)

        def norm_rope(t, w):
            tn = _rms(t) * w
            return (tn * cos + pltpu.roll(tn, HEAD_DIM - AXIS_ROT // 2, 1) * s_lo
                    + pltpu.roll(tn, AXIS_ROT // 2, 1) * s_hi)

        def proj(lo, width):
            return jnp.dot(h_ref[rows, :], w_ref[:, lo:lo + width], preferred_element_type=_F32)

        for c in range(ATTN_W // pair):
            acc = proj(c * pair, pair)
            for hh in range(2):
                t = norm_rope(acc[:, hh * HEAD_DIM:(hh + 1) * HEAD_DIM], qw) * QK_SCALE_LOG2
                q_ref[2 * c + hh, rows, :] = t.astype(_BF16)
        acc = proj(ATTN_W, KV_W)
        for hh in range(N_KV_HEADS):
            t = norm_rope(acc[:, hh * HEAD_DIM:(hh + 1) * HEAD_DIM], kw)
            k_ref[rows, hh * HEAD_DIM:(hh + 1) * HEAD_DIM] = t.astype(_BF16)
        acc = proj(ATTN_W + KV_W, KV_W)
        v_ref[rows, :] = acc.astype(_BF16)
        vt_ref[:, rows] = acc.T.astype(_BF16)
        for c in range(ATTN_W // 512):
            g_ref[rows, c * 512:(c + 1) * 512] = _silu(proj(g0 + c * 512, 512)).astype(_BF16)


def _attn_proj(xf, nw, w_in, layer, qw, kw, rope_row, rope_col, seq_len):
    n = xf.shape[0]
    tm = PROJ_TM
    assert tm // GRID_W == SUBLANES
    tiles_per_seq = seq_len // tm
    row = lambda i: (i, 0)
    const = lambda i: (0, 0)
    return pl.pallas_call(
        _attn_proj_kernel,
        grid=(n // tm,),
        in_specs=[
            pl.BlockSpec((tm, D_MODEL), row),
            pl.BlockSpec((1, D_MODEL), const),
            pl.BlockSpec((None, D_MODEL, ATTN_PROJ_W), lambda i: (layer, 0, 0), pipeline_mode=pl.Buffered(1)),
            pl.BlockSpec((1, HEAD_DIM), const),
            pl.BlockSpec((1, HEAD_DIM), const),
            pl.BlockSpec((3, SUBLANES, HEAD_DIM), lambda i: (0, i % tiles_per_seq, 0)),
            pl.BlockSpec((3, GRID_W, HEAD_DIM), lambda i: (0, 0, 0)),
        ],
        out_specs=[
            pl.BlockSpec((N_Q_HEADS, tm, HEAD_DIM), lambda i: (0, i, 0)),
            pl.BlockSpec((tm, KV_W), row),
            pl.BlockSpec((tm, KV_W), row),
            pl.BlockSpec((KV_W, tm), lambda i: (0, i)),
            pl.BlockSpec((tm, ATTN_W), row),
            pl.BlockSpec((tm, D_MODEL), row),
        ],
        out_shape=[
            jax.ShapeDtypeStruct((N_Q_HEADS, n, HEAD_DIM), _BF16),
            jax.ShapeDtypeStruct((n, KV_W), _BF16),
            jax.ShapeDtypeStruct((n, KV_W), _BF16),
            jax.ShapeDtypeStruct((KV_W, n), _BF16),
            jax.ShapeDtypeStruct((n, ATTN_W), _BF16),
            jax.ShapeDtypeStruct((n, D_MODEL), _BF16),
        ],
        compiler_params=pltpu.CompilerParams(
            dimension_semantics=("parallel",), vmem_limit_bytes=VMEM_LIMIT_BYTES),
        name="attn_proj",
    )(xf, nw, w_in, qw, kw, rope_row, rope_col)


def _rest_proj_kernel(h_ref, *refs):
    w_refs = refs[:REST_BLOCKS]
    snw_ref, sw_ref, sb_ref, hc_ref, cb_ref, gc_ref, os_ref, gs_ref = refs[REST_BLOCKS:]
    tm = h_ref.shape[0]

    def proj(idx):
        return jnp.dot(h_ref[...], w_refs[idx][...], preferred_element_type=_F32)

    u = _gelu(proj(4))
    v = _gelu(proj(5))
    hc_ref[...] = proj(2) * proj(0)
    n_chunks = tm // CHUNK
    for g in range(SGU_GROUPS):
        lo = g * SGU_GC
        vn = (_rms(v[:, lo:lo + SGU_GC]) * snw_ref[:, lo:lo + SGU_GC]).astype(_BF16)
        rhs = jnp.concatenate([vn[c * CHUNK:(c + 1) * CHUNK, :] for c in range(n_chunks)], axis=1)
        s = jnp.dot(sw_ref[g], rhs, preferred_element_type=_F32) + sb_ref[:, g:g + 1]
        for c in range(n_chunks):
            os_ref[c * CHUNK:(c + 1) * CHUNK, lo:lo + SGU_GC] = (
                u[c * CHUNK:(c + 1) * CHUNK, lo:lo + SGU_GC] * s[:, c * SGU_GC:(c + 1) * SGU_GC]
            ).astype(_BF16)
    gc_ref[...] = _silu(proj(3)).astype(_BF16)
    gs_ref[...] = _silu(proj(6)).astype(_BF16)
    cb_ref[...] = proj(1).astype(_BF16)


def _rest_proj(h, w_in, layer, snw, sw, sb_t):
    n = h.shape[0]
    tm = PROJ_TM
    row = lambda i: (i, 0)
    const = lambda i: (0, 0)
    outs = [jax.ShapeDtypeStruct((n, REST_BLOCK_W), dt) for dt in (_F32, _BF16, _BF16, _BF16, _BF16)]
    first_block = ATTN_PROJ_W // REST_BLOCK_W
    w_specs = [
        pl.BlockSpec((None, D_MODEL, REST_BLOCK_W), (lambda i, j=j: (layer, 0, first_block + j)),
                     pipeline_mode=pl.Buffered(1))
        for j in range(REST_BLOCKS)
    ]
    return pl.pallas_call(
        _rest_proj_kernel,
        grid=(n // tm,),
        in_specs=[
            pl.BlockSpec((tm, D_MODEL), row),
            *w_specs,
            pl.BlockSpec((1, SGU_W), const),
            pl.BlockSpec((SGU_GROUPS, CHUNK, CHUNK), lambda i: (0, 0, 0)),
            pl.BlockSpec((CHUNK, SGU_GROUPS), const),
        ],
        out_specs=[pl.BlockSpec((tm, REST_BLOCK_W), row)] * 5,
        out_shape=outs,
        compiler_params=pltpu.CompilerParams(
            dimension_semantics=("parallel",), vmem_limit_bytes=VMEM_LIMIT_BYTES),
        name="rest_proj",
    )(h, *([w_in] * REST_BLOCKS), snw, sw, sb_t)


def _flash_online_kernel(q_ref, k_ref, v_ref, o_ref, s_scr, p_scr, m_scr, l_scr, acc_scr, *, tk, rb):
    tq = q_ref.shape[1]
    n_chunks = k_ref.shape[0] // tk

    m_scr[...] = jnp.full(m_scr.shape, -jnp.inf, _F32)
    l_scr[...] = jnp.zeros(l_scr.shape, _F32)
    acc_scr[...] = jnp.zeros(acc_scr.shape, _F32)
    p_scr[1] = jnp.zeros(p_scr.shape[1:], _BF16)

    def kv_rows(c):
        return pl.ds(pl.multiple_of(c * tk, tk), tk)

    def qk(g, c):
        return lax.dot_general(q_ref[g], k_ref[kv_rows(c), :], (((1,), (1,)), ((), ())),
                               preferred_element_type=_F32)

    def pv(g, slot, c):
        acc_scr[g] += jnp.dot(p_scr[slot], v_ref[kv_rows(c), :], preferred_element_type=_F32)

    def softmax(g, slot):
        for r in range(tq // rb):
            rows = slice(r * rb, (r + 1) * rb)
            s = s_scr[slot, rows, :]
            m_prev = m_scr[g, rows, :]
            m_next = jnp.maximum(m_prev, jnp.max(s, axis=1, keepdims=True))
            alpha = jnp.exp2(m_prev - m_next)
            p = jnp.exp2(s - jnp.tile(m_next, (1, tk // LANES)))
            l_scr[g, rows, :] = alpha * l_scr[g, rows, :] + jnp.sum(p, axis=1, keepdims=True)
            m_scr[g, rows, :] = m_next
            acc_scr[g, rows, :] = alpha * acc_scr[g, rows, :]
            p_scr[slot, rows, :] = p.astype(_BF16)

    s_scr[0] = qk(0, 0)

    def body(c, carry):
        for g in range(Q_PER_KV):
            slot = g % 2
            if g < Q_PER_KV - 1:
                s_scr[1 - slot] = qk(g + 1, c)
            else:
                s_scr[1 - slot] = qk(0, jnp.minimum(c + 1, n_chunks - 1))
            softmax(g, slot)
            if g == 0:
                pv(Q_PER_KV - 1, 1, jnp.maximum(c - 1, 0))
            else:
                pv(g - 1, 1 - slot, c)
        return carry

    lax.fori_loop(0, n_chunks, body, 0)
    pv(Q_PER_KV - 1, 1, n_chunks - 1)

    for g in range(Q_PER_KV):
        o_ref[:, g * HEAD_DIM:(g + 1) * HEAD_DIM] = (acc_scr[g] / l_scr[g]).astype(_BF16)


def _flash_online(q, k, v, batch, seq_len):
    tq, tk = ONLINE_TQ, ONLINE_TK
    q_tiles = seq_len // tq
    group_w = Q_PER_KV * HEAD_DIM
    kern = functools.partial(_flash_online_kernel, tk=tk, rb=SOFTMAX_ROWS)
    return pl.pallas_call(
        kern,
        grid=(batch, N_KV_HEADS, q_tiles),
        in_specs=[
            pl.BlockSpec((Q_PER_KV, tq, HEAD_DIM), lambda bi, hi, qi: (hi, bi * q_tiles + qi, 0)),
            pl.BlockSpec((seq_len, HEAD_DIM), lambda bi, hi, qi: (bi, hi)),
            pl.BlockSpec((seq_len, HEAD_DIM), lambda bi, hi, qi: (bi, hi)),
        ],
        out_specs=pl.BlockSpec((tq, group_w), lambda bi, hi, qi: (bi * q_tiles + qi, hi)),
        out_shape=jax.ShapeDtypeStruct((batch * seq_len, ATTN_W), _BF16),
        scratch_shapes=[
            pltpu.VMEM((2, tq, tk), _F32),
            pltpu.VMEM((2, tq, tk), _BF16),
            pltpu.VMEM((Q_PER_KV, tq, LANES), _F32),
            pltpu.VMEM((Q_PER_KV, tq, LANES), _F32),
            pltpu.VMEM((Q_PER_KV, tq, HEAD_DIM), _F32),
        ],
        compiler_params=pltpu.CompilerParams(
            dimension_semantics=("parallel", "parallel", "parallel"),
            vmem_limit_bytes=VMEM_LIMIT_BYTES),
        name="flash_online",
    )(q, k, v)

def _flash_bounded_kernel(shift_ref, q_ref, k_ref, vt_ref, o_ref, p_scr, l_scr, acc_scr, *, tk):
    tq = q_ref.shape[1]
    n_chunks = k_ref.shape[0] // tk
    shift = shift_ref[...]

    def kv_rows(c):
        return pl.ds(pl.multiple_of(c * tk, tk), tk)

    def probs(g, c, slot, first):
        s = lax.dot_general(k_ref[kv_rows(c), :], q_ref[g], (((1,), (1,)), ((), ())),
                            preferred_element_type=_F32)
        p = jnp.exp2(s - shift)
        part = jnp.sum(p.reshape(tk // SUBLANES, SUBLANES, tq), axis=0)
        if first:
            l_scr[g] = part
        else:
            l_scr[g] += part
        p_scr[slot] = p.astype(_BF16)

    def pv(g, slot, c, first):
        d = jnp.dot(vt_ref[:, kv_rows(c)], p_scr[slot], preferred_element_type=_F32)
        if first:
            acc_scr[g] = d
        else:
            acc_scr[g] += d

    def chunk(c, first=False, last=False):
        for g in range(Q_PER_KV):
            slot = g % 2
            if g < Q_PER_KV - 1:
                probs(g + 1, c, 1 - slot, first)
            elif not last:
                probs(0, c + 1, 1 - slot, False)
            pv(g, slot, c, first)

    probs(0, 0, 0, True)
    chunk(0, first=True)

    def body(c, carry):
        chunk(c)
        return carry

    lax.fori_loop(1, n_chunks - 1, body, 0)
    chunk(n_chunks - 1, last=True)

    for g in range(Q_PER_KV):
        l = jnp.sum(l_scr[g], axis=0, keepdims=True)
        o_ref[:, g * HEAD_DIM:(g + 1) * HEAD_DIM] = (acc_scr[g] / l).T.astype(_BF16)


def _flash_bounded(q, k, vt, shift, batch, seq_len):
    tq, tk = BOUNDED_TQ, BOUNDED_TK
    assert seq_len // tk >= 2
    q_tiles = seq_len // tq
    group_w = Q_PER_KV * HEAD_DIM
    kern = functools.partial(_flash_bounded_kernel, tk=tk)
    return pl.pallas_call(
        kern,
        grid=(batch, N_KV_HEADS, q_tiles),
        in_specs=[
            pl.BlockSpec((1, tq), lambda bi, hi, qi: (0, 0)),
            pl.BlockSpec((Q_PER_KV, tq, HEAD_DIM), lambda bi, hi, qi: (hi, bi * q_tiles + qi, 0)),
            pl.BlockSpec((seq_len, HEAD_DIM), lambda bi, hi, qi: (bi, hi)),
            pl.BlockSpec((HEAD_DIM, seq_len), lambda bi, hi, qi: (hi, bi)),
        ],
        out_specs=pl.BlockSpec((tq, group_w), lambda bi, hi, qi: (bi * q_tiles + qi, hi)),
        out_shape=jax.ShapeDtypeStruct((batch * seq_len, ATTN_W), _BF16),
        scratch_shapes=[
            pltpu.VMEM((2, tk, tq), _BF16),
            pltpu.VMEM((Q_PER_KV, SUBLANES, tq), _F32),
            pltpu.VMEM((Q_PER_KV, HEAD_DIM, tq), _F32),
        ],
        compiler_params=pltpu.CompilerParams(
            dimension_semantics=("parallel", "parallel", "parallel"),
            vmem_limit_bytes=VMEM_LIMIT_BYTES),
        name="flash_bounded",
    )(shift, q, k, vt)


def _score_bound(q_norm_w, k_norm_w):
    rounding = (1.0 + 2.0 ** -8) ** 2
    return (HEAD_DIM * QK_SCALE_LOG2 * rounding) * jnp.max(jnp.abs(q_norm_w)) * jnp.max(jnp.abs(k_norm_w))


def _attention(q, k, v, vt, q_norm_w, k_norm_w, batch, seq_len):
    shift = _score_bound(q_norm_w, k_norm_w)
    return lax.cond(
        shift <= MAX_BOUNDED_SHIFT,
        lambda: _flash_bounded(q, k, vt, jnp.full((1, BOUNDED_TQ), shift, _F32), batch, seq_len),
        lambda: _flash_online(q, k, v, batch, seq_len))


def _mix_kernel(x_ref, oa_ref, ga_ref, hc_ref, hp_ref, hn_ref, cb_ref, gc_ref, os_ref, gs_ref,
                cw_ref, bw_ref, w_ref, fw_ref, out_ref, mixed_scr, *, tiles_per_seq, final_norm):
    i = pl.program_id(0)
    tm = x_ref.shape[0]
    f32 = lambda ref: ref[...].astype(_F32)
    mixed_scr[:, :ATTN_W] = (_rms(f32(oa_ref)) * bw_ref[:, :ATTN_W] * f32(ga_ref)).astype(_BF16)

    hc = hc_ref[...]
    t = i % tiles_per_seq
    prev_row = jnp.where(t == 0, 0.0, hp_ref[SUBLANES - 1:SUBLANES, :])
    next_row = jnp.where(t == tiles_per_seq - 1, 0.0, hn_ref[0:1, :])
    ridx = lax.broadcasted_iota(jnp.int32, hc.shape, 0)
    up = jnp.where(ridx == 0, prev_row, pltpu.roll(hc, 1, 0))
    down = jnp.where(ridx == tm - 1, next_row, pltpu.roll(hc, tm - 1, 0))
    y = f32(cb_ref) * (up * cw_ref[0:1, :] + hc * cw_ref[1:2, :] + down * cw_ref[2:3, :])
    c0 = ATTN_W
    mixed_scr[:, c0:c0 + CONV_W] = (_rms(y) * bw_ref[:, c0:c0 + CONV_W] * f32(gc_ref)).astype(_BF16)
    c1 = ATTN_W + CONV_W
    mixed_scr[:, c1:] = (_rms(f32(os_ref)) * bw_ref[:, c1:] * f32(gs_ref)).astype(_BF16)

    out = x_ref[...] + jnp.dot(mixed_scr[...], w_ref[...], preferred_element_type=_F32)
    if final_norm:
        out = _rms(out) * fw_ref[...]
    out_ref[...] = out


def _mix(xf, oa, ga, hc, cb, gc, osg, gs, cw_t, bw, w_out, layer, fw, seq_len, final_norm):
    n = xf.shape[0]
    tm = MIX_TM
    tiles_per_seq = seq_len // tm
    halo_blocks = tm // SUBLANES
    n_halo = n // SUBLANES
    row = lambda i: (i, 0)
    const = lambda i: (0, 0)
    prev = lambda i: (jnp.maximum(i * halo_blocks - 1, 0), 0)
    nxt = lambda i: (jnp.minimum((i + 1) * halo_blocks, n_halo - 1), 0)
    kern = functools.partial(_mix_kernel, tiles_per_seq=tiles_per_seq, final_norm=final_norm)
    return pl.pallas_call(
        kern,
        grid=(n // tm,),
        in_specs=[
            pl.BlockSpec((tm, D_MODEL), row),
            pl.BlockSpec((tm, ATTN_W), row),
            pl.BlockSpec((tm, ATTN_W), row),
            pl.BlockSpec((tm, CONV_W), row),
            pl.BlockSpec((SUBLANES, CONV_W), prev),
            pl.BlockSpec((SUBLANES, CONV_W), nxt),
            pl.BlockSpec((tm, CONV_W), row),
            pl.BlockSpec((tm, CONV_W), row),
            pl.BlockSpec((tm, SGU_W), row),
            pl.BlockSpec((tm, SGU_W), row),
            pl.BlockSpec((3, CONV_W), const),
            pl.BlockSpec((1, MIX_W), const),
            pl.BlockSpec((None, MIX_W, D_MODEL), lambda i: (layer, 0, 0), pipeline_mode=pl.Buffered(1)),
            pl.BlockSpec((1, D_MODEL), const),
        ],
        out_specs=pl.BlockSpec((tm, D_MODEL), row),
        out_shape=jax.ShapeDtypeStruct((n, D_MODEL), _F32),
        scratch_shapes=[pltpu.VMEM((tm, MIX_W), _BF16)],
        compiler_params=pltpu.CompilerParams(
            dimension_semantics=("parallel",), vmem_limit_bytes=VMEM_LIMIT_BYTES),
        name="mix_out",
    )(xf, oa, ga, hc, hc, hc, cb, gc, osg, gs, cw_t, bw, w_out, fw)


def kernel(x, norm_w, w_in, q_norm_w, k_norm_w, conv_w, sgu_norm_w, sgu_w, sgu_b, branch_norm_w, w_out,
           final_norm_w):
    b, s, d = x.shape
    depth = w_in.shape[0]
    n = b * s
    rope_row, rope_col = _rope_tables(s)
    sw = sgu_w.astype(_BF16)
    xf = x.reshape(n, d)
    for l in range(depth):
        q, k, v, vt, ga, h = _attn_proj(xf, norm_w[l].reshape(1, d), w_in, l, q_norm_w[l].reshape(1, HEAD_DIM),
                                        k_norm_w[l].reshape(1, HEAD_DIM), rope_row, rope_col, s)
        hc, cb, gc, osg, gs = _rest_proj(h, w_in, l, sgu_norm_w[l].reshape(1, SGU_W), sw[l], sgu_b[l].T)
        oa = _attention(q, k, v, vt, q_norm_w[l], k_norm_w[l], b, s)
        xf = _mix(xf, oa, ga, hc, cb, gc, osg, gs, conv_w[l].T, branch_norm_w[l].reshape(1, MIX_W),
                  w_out, l, final_norm_w.reshape(1, d), s, final_norm=(l == depth - 1))
    return xf.reshape(b, s, d)
```

```python
import functools

import jax
import jax.numpy as jnp
import numpy as np
from jax import lax
from jax.experimental import pallas as pl
from jax.experimental.pallas import tpu as pltpu

D_MODEL = 2048
HEAD_DIM = 128
N_Q_HEADS = 8
N_KV_HEADS = 2
Q_PER_KV = N_Q_HEADS // N_KV_HEADS
ATTN_W = N_Q_HEADS * HEAD_DIM
KV_W = N_KV_HEADS * HEAD_DIM
CONV_W = 512
SGU_GROUPS = 4
SGU_GC = 128
SGU_W = SGU_GROUPS * SGU_GC
CHUNK = 128
GRID_W = 64
ROPE_THETA = 10000.0
AXIS_ROT = HEAD_DIM // 2
EPS = 1e-6
MIX_W = ATTN_W + CONV_W + SGU_W
ATTN_PROJ_W = 2 * ATTN_W + 2 * KV_W
REST_BLOCK_W = 512
REST_BLOCKS = 7
assert CONV_W == SGU_W == REST_BLOCK_W and ATTN_PROJ_W % REST_BLOCK_W == 0
QK_SCALE_LOG2 = float(HEAD_DIM ** -0.5 * np.log2(np.e))

LANES = 128
SUBLANES = 8
VMEM_LIMIT_BYTES = 56 * 1024 * 1024

PROJ_TM = 512
MIX_TM = 512
BOUNDED_TQ = 1024
ONLINE_TQ = 512
BOUNDED_TK = 2048
ONLINE_TK = 512
SOFTMAX_ROWS = 64
assert Q_PER_KV % 2 == 0

MAX_BOUNDED_SHIFT = 48.0

_BF16 = jnp.bfloat16
_F32 = jnp.float32


def _rms(t):
    return t * lax.rsqrt(jnp.mean(t * t, axis=-1, keepdims=True) + EPS)


def _silu(t):
    return t * (1.0 / (1.0 + jnp.exp(-t)))


def _gelu(t):
    return 0.5 * t * (1.0 + lax.erf(t * np.float32(np.sqrt(0.5))))


def _rope_tables(seq_len):
    inv_freq = ROPE_THETA ** (-jnp.arange(0, AXIS_ROT, 2, dtype=_F32) / AXIS_ROT)
    first_half = jnp.arange(AXIS_ROT) < (AXIS_ROT // 2)

    def axis_table(count):
        ang = jnp.arange(count, dtype=_F32)[:, None] * inv_freq
        ang = jnp.concatenate([ang, ang], axis=1)
        cos, sin = jnp.cos(ang), jnp.sin(ang)
        return jnp.stack([cos, jnp.where(first_half, -sin, 0.0), jnp.where(first_half, 0.0, sin)])

    row_t, col_t = axis_table(seq_len // GRID_W), axis_table(GRID_W)
    rope_row = jnp.concatenate([row_t, jnp.zeros_like(row_t)], axis=-1)
    rope_col = jnp.concatenate([jnp.zeros_like(col_t), col_t], axis=-1)
    return rope_row, rope_col


def _attn_proj_kernel(x_ref, nw_ref, w_ref, qw_ref, kw_ref, rrow_ref, rcol_ref,
                      q_ref, k_ref, v_ref, vt_ref, g_ref, h_ref):
    tm = x_ref.shape[0]
    half = tm // 2
    grid_rows = half // GRID_W
    qw, kw = qw_ref[...], kw_ref[...]
    pair = 2 * HEAD_DIM
    g0 = ATTN_W + 2 * KV_W

    def rope_table(which, hf):
        return jnp.concatenate(
            [rrow_ref[which, hf * grid_rows + g:hf * grid_rows + g + 1, :] + rcol_ref[which]
             for g in range(grid_rows)], axis=0)

    for hf in range(2):
        rows = slice(hf * half, (hf + 1) * half)
        h_ref[rows, :] = (_rms(x_ref[rows, :]) * nw_ref[...]).astype(_BF16)
        cos, s_lo, s_hi = rope_table(0, hf), rope_table(1, hf), rope_table(2, hf)

        def norm_rope(t, w):
            tn = _rms(t) * w
            return (tn * cos + pltpu.roll(tn, HEAD_DIM - AXIS_ROT // 2, 1) * s_lo
                    + pltpu.roll(tn, AXIS_ROT // 2, 1) * s_hi)

        def proj(lo, width):
            return jnp.dot(h_ref[rows, :], w_ref[:, lo:lo + width], preferred_element_type=_F32)

        for c in range(ATTN_W // pair):
            acc = proj(c * pair, pair)
            for hh in range(2):
                t = norm_rope(acc[:, hh * HEAD_DIM:(hh + 1) * HEAD_DIM], qw) * QK_SCALE_LOG2
                q_ref[2 * c + hh, rows, :] = t.astype(_BF16)
        acc = proj(ATTN_W, KV_W)
        for hh in range(N_KV_HEADS):
            t = norm_rope(acc[:, hh * HEAD_DIM:(hh + 1) * HEAD_DIM], kw)
            k_ref[rows, hh * HEAD_DIM:(hh + 1) * HEAD_DIM] = t.astype(_BF16)
        acc = proj(ATTN_W + KV_W, KV_W)
        v_ref[rows, :] = acc.astype(_BF16)
        vt_ref[:, rows] = acc.T.astype(_BF16)
        for c in range(ATTN_W // 512):
            g_ref[rows, c * 512:(c + 1) * 512] = _silu(proj(g0 + c * 512, 512)).astype(_BF16)


def _attn_proj(xf, nw, w_in, layer, qw, kw, rope_row, rope_col, seq_len):
    n = xf.shape[0]
    tm = PROJ_TM
    assert tm // GRID_W == SUBLANES
    tiles_per_seq = seq_len // tm
    row = lambda i: (i, 0)
    const = lambda i: (0, 0)
    return pl.pallas_call(
        _attn_proj_kernel,
        grid=(n // tm,),
        in_specs=[
            pl.BlockSpec((tm, D_MODEL), row),
            pl.BlockSpec((1, D_MODEL), const),
            pl.BlockSpec((None, D_MODEL, ATTN_PROJ_W), lambda i: (layer, 0, 0), pipeline_mode=pl.Buffered(1)),
            pl.BlockSpec((1, HEAD_DIM), const),
            pl.BlockSpec((1, HEAD_DIM), const),
            pl.BlockSpec((3, SUBLANES, HEAD_DIM), lambda i: (0, i % tiles_per_seq, 0)),
            pl.BlockSpec((3, GRID_W, HEAD_DIM), lambda i: (0, 0, 0)),
        ],
        out_specs=[
            pl.BlockSpec((N_Q_HEADS, tm, HEAD_DIM), lambda i: (0, i, 0)),
            pl.BlockSpec((tm, KV_W), row),
            pl.BlockSpec((tm, KV_W), row),
            pl.BlockSpec((KV_W, tm), lambda i: (0, i)),
            pl.BlockSpec((tm, ATTN_W), row),
            pl.BlockSpec((tm, D_MODEL), row),
        ],
        out_shape=[
            jax.ShapeDtypeStruct((N_Q_HEADS, n, HEAD_DIM), _BF16),
            jax.ShapeDtypeStruct((n, KV_W), _BF16),
            jax.ShapeDtypeStruct((n, KV_W), _BF16),
            jax.ShapeDtypeStruct((KV_W, n), _BF16),
            jax.ShapeDtypeStruct((n, ATTN_W), _BF16),
            jax.ShapeDtypeStruct((n, D_MODEL), _BF16),
        ],
        compiler_params=pltpu.CompilerParams(
            dimension_semantics=("parallel",), vmem_limit_bytes=VMEM_LIMIT_BYTES),
        name="attn_proj",
    )(xf, nw, w_in, qw, kw, rope_row, rope_col)


def _rest_proj_kernel(h_ref, *refs):
    w_refs = refs[:REST_BLOCKS]
    snw_ref, sw_ref, sb_ref, hc_ref, cb_ref, gc_ref, os_ref, gs_ref = refs[REST_BLOCKS:]
    tm = h_ref.shape[0]

    def proj(idx):
        return jnp.dot(h_ref[...], w_refs[idx][...], preferred_element_type=_F32)

    u = _gelu(proj(4))
    v = _gelu(proj(5))
    hc_ref[...] = proj(2) * proj(0)
    n_chunks = tm // CHUNK
    for g in range(SGU_GROUPS):
        lo = g * SGU_GC
        vn = (_rms(v[:, lo:lo + SGU_GC]) * snw_ref[:, lo:lo + SGU_GC]).astype(_BF16)
        rhs = jnp.concatenate([vn[c * CHUNK:(c + 1) * CHUNK, :] for c in range(n_chunks)], axis=1)
        s = jnp.dot(sw_ref[g], rhs, preferred_element_type=_F32) + sb_ref[:, g:g + 1]
        for c in range(n_chunks):
            os_ref[c * CHUNK:(c + 1) * CHUNK, lo:lo + SGU_GC] = (
                u[c * CHUNK:(c + 1) * CHUNK, lo:lo + SGU_GC] * s[:, c * SGU_GC:(c + 1) * SGU_GC]
            ).astype(_BF16)
    gc_ref[...] = _silu(proj(3)).astype(_BF16)
    gs_ref[...] = _silu(proj(6)).astype(_BF16)
    cb_ref[...] = proj(1).astype(_BF16)


def _rest_proj(h, w_in, layer, snw, sw, sb_t):
    n = h.shape[0]
    tm = PROJ_TM
    row = lambda i: (i, 0)
    const = lambda i: (0, 0)
    outs = [jax.ShapeDtypeStruct((n, REST_BLOCK_W), dt) for dt in (_F32, _BF16, _BF16, _BF16, _BF16)]
    first_block = ATTN_PROJ_W // REST_BLOCK_W
    w_specs = [
        pl.BlockSpec((None, D_MODEL, REST_BLOCK_W), (lambda i, j=j: (layer, 0, first_block + j)),
                     pipeline_mode=pl.Buffered(1))
        for j in range(REST_BLOCKS)
    ]
    return pl.pallas_call(
        _rest_proj_kernel,
        grid=(n // tm,),
        in_specs=[
            pl.BlockSpec((tm, D_MODEL), row),
            *w_specs,
            pl.BlockSpec((1, SGU_W), const),
            pl.BlockSpec((SGU_GROUPS, CHUNK, CHUNK), lambda i: (0, 0, 0)),
            pl.BlockSpec((CHUNK, SGU_GROUPS), const),
        ],
        out_specs=[pl.BlockSpec((tm, REST_BLOCK_W), row)] * 5,
        out_shape=outs,
        compiler_params=pltpu.CompilerParams(
            dimension_semantics=("parallel",), vmem_limit_bytes=VMEM_LIMIT_BYTES),
        name="rest_proj",
    )(h, *([w_in] * REST_BLOCKS), snw, sw, sb_t)


def _flash_online_kernel(q_ref, k_ref, v_ref, o_ref, s_scr, p_scr, m_scr, l_scr, acc_scr, *, tk, rb):
    tq = q_ref.shape[1]
    n_chunks = k_ref.shape[0] // tk

    m_scr[...] = jnp.full(m_scr.shape, -jnp.inf, _F32)
    l_scr[...] = jnp.zeros(l_scr.shape, _F32)
    acc_scr[...] = jnp.zeros(acc_scr.shape, _F32)
    p_scr[1] = jnp.zeros(p_scr.shape[1:], _BF16)

    def kv_rows(c):
        return pl.ds(pl.multiple_of(c * tk, tk), tk)

    def qk(g, c):
        return lax.dot_general(q_ref[g], k_ref[kv_rows(c), :], (((1,), (1,)), ((), ())),
                               preferred_element_type=_F32)

    def pv(g, slot, c):
        acc_scr[g] += jnp.dot(p_scr[slot], v_ref[kv_rows(c), :], preferred_element_type=_F32)

    def softmax(g, slot):
        for r in range(tq // rb):
            rows = slice(r * rb, (r + 1) * rb)
            s = s_scr[slot, rows, :]
            m_prev = m_scr[g, rows, :]
            m_next = jnp.maximum(m_prev, jnp.max(s, axis=1, keepdims=True))
            alpha = jnp.exp2(m_prev - m_next)
            p = jnp.exp2(s - jnp.tile(m_next, (1, tk // LANES)))
            l_scr[g, rows, :] = alpha * l_scr[g, rows, :] + jnp.sum(p, axis=1, keepdims=True)
            m_scr[g, rows, :] = m_next
            acc_scr[g, rows, :] = alpha * acc_scr[g, rows, :]
            p_scr[slot, rows, :] = p.astype(_BF16)

    s_scr[0] = qk(0, 0)

    def body(c, carry):
        for g in range(Q_PER_KV):
            slot = g % 2
            if g < Q_PER_KV - 1:
                s_scr[1 - slot] = qk(g + 1, c)
            else:
                s_scr[1 - slot] = qk(0, jnp.minimum(c + 1, n_chunks - 1))
            softmax(g, slot)
            if g == 0:
                pv(Q_PER_KV - 1, 1, jnp.maximum(c - 1, 0))
            else:
                pv(g - 1, 1 - slot, c)
        return carry

    lax.fori_loop(0, n_chunks, body, 0)
    pv(Q_PER_KV - 1, 1, n_chunks - 1)

    for g in range(Q_PER_KV):
        o_ref[:, g * HEAD_DIM:(g + 1) * HEAD_DIM] = (acc_scr[g] / l_scr[g]).astype(_BF16)


def _flash_online(q, k, v, batch, seq_len):
    tq, tk = ONLINE_TQ, ONLINE_TK
    q_tiles = seq_len // tq
    group_w = Q_PER_KV * HEAD_DIM
    kern = functools.partial(_flash_online_kernel, tk=tk, rb=SOFTMAX_ROWS)
    return pl.pallas_call(
        kern,
        grid=(batch, N_KV_HEADS, q_tiles),
        in_specs=[
            pl.BlockSpec((Q_PER_KV, tq, HEAD_DIM), lambda bi, hi, qi: (hi, bi * q_tiles + qi, 0)),
            pl.BlockSpec((seq_len, HEAD_DIM), lambda bi, hi, qi: (bi, hi)),
            pl.BlockSpec((seq_len, HEAD_DIM), lambda bi, hi, qi: (bi, hi)),
        ],
        out_specs=pl.BlockSpec((tq, group_w), lambda bi, hi, qi: (bi * q_tiles + qi, hi)),
        out_shape=jax.ShapeDtypeStruct((batch * seq_len, ATTN_W), _BF16),
        scratch_shapes=[
            pltpu.VMEM((2, tq, tk), _F32),
            pltpu.VMEM((2, tq, tk), _BF16),
            pltpu.VMEM((Q_PER_KV, tq, LANES), _F32),
            pltpu.VMEM((Q_PER_KV, tq, LANES), _F32),
            pltpu.VMEM((Q_PER_KV, tq, HEAD_DIM), _F32),
        ],
        compiler_params=pltpu.CompilerParams(
            dimension_semantics=("parallel", "parallel", "parallel"),
            vmem_limit_bytes=VMEM_LIMIT_BYTES),
        name="flash_online",
    )(q, k, v)

def _flash_bounded_kernel(shift_ref, q_ref, k_ref, vt_ref, o_ref, p_scr, l_scr, acc_scr, *, tq, tk):
    seq_len = k_ref.shape[0]
    n_chunks = seq_len // tk
    shift = shift_ref[...]

    def kv_rows(c):
        return pl.ds(pl.multiple_of(c * tk, tk), tk)

    def q_tile(qt, carry):
        q_rows = pl.ds(pl.multiple_of(qt * tq, tq), tq)

        def probs(g, c, slot, first):
            s = lax.dot_general(k_ref[kv_rows(c), :], q_ref[g, q_rows, :], (((1,), (1,)), ((), ())),
                                preferred_element_type=_F32)
            p = jnp.exp2(s - shift)
            part = jnp.sum(p.reshape(tk // SUBLANES, SUBLANES, tq), axis=0)
            if first:
                l_scr[g] = part
            else:
                l_scr[g] += part
            p_scr[slot] = p.astype(_BF16)

        def pv(g, slot, c, first):
            d = jnp.dot(vt_ref[:, kv_rows(c)], p_scr[slot], preferred_element_type=_F32)
            if first:
                acc_scr[g] = d
            else:
                acc_scr[g] += d

        def chunk(c, first=False, last=False):
            for g in range(Q_PER_KV):
                slot = g % 2
                if g < Q_PER_KV - 1:
                    probs(g + 1, c, 1 - slot, first)
                elif not last:
                    probs(0, c + 1, 1 - slot, False)
                pv(g, slot, c, first)

        def body(c, inner):
            chunk(c)
            return inner

        probs(0, 0, 0, True)
        chunk(0, first=True)
        lax.fori_loop(1, n_chunks - 1, body, 0)
        chunk(n_chunks - 1, last=True)

        for g in range(Q_PER_KV):
            l = jnp.sum(l_scr[g], axis=0, keepdims=True)
            o_ref[q_rows, g * HEAD_DIM:(g + 1) * HEAD_DIM] = (acc_scr[g] / l).T.astype(_BF16)
        return carry

    lax.fori_loop(0, seq_len // tq, q_tile, 0)


def _flash_bounded(q, k, vt, shift, batch, seq_len):
    tq, tk = BOUNDED_TQ, BOUNDED_TK
    assert seq_len // tk >= 2
    group_w = Q_PER_KV * HEAD_DIM
    kern = functools.partial(_flash_bounded_kernel, tq=tq, tk=tk)
    return pl.pallas_call(
        kern,
        grid=(batch, N_KV_HEADS),
        in_specs=[
            pl.BlockSpec((1, tq), lambda bi, hi: (0, 0)),
            pl.BlockSpec((Q_PER_KV, seq_len, HEAD_DIM), lambda bi, hi: (hi, bi, 0)),
            pl.BlockSpec((seq_len, HEAD_DIM), lambda bi, hi: (bi, hi)),
            pl.BlockSpec((HEAD_DIM, seq_len), lambda bi, hi: (hi, bi)),
        ],
        out_specs=pl.BlockSpec((seq_len, group_w), lambda bi, hi: (bi, hi)),
        out_shape=jax.ShapeDtypeStruct((batch * seq_len, ATTN_W), _BF16),
        scratch_shapes=[
            pltpu.VMEM((2, tk, tq), _BF16),
            pltpu.VMEM((Q_PER_KV, SUBLANES, tq), _F32),
            pltpu.VMEM((Q_PER_KV, HEAD_DIM, tq), _F32),
        ],
        compiler_params=pltpu.CompilerParams(
            dimension_semantics=("parallel", "parallel"),
            vmem_limit_bytes=VMEM_LIMIT_BYTES),
        name="flash_bounded",
    )(shift, q, k, vt)


def _score_bound(q_norm_w, k_norm_w):
    rounding = (1.0 + 2.0 ** -8) ** 2
    return (HEAD_DIM * QK_SCALE_LOG2 * rounding) * jnp.max(jnp.abs(q_norm_w)) * jnp.max(jnp.abs(k_norm_w))


def _attention(q, k, v, vt, q_norm_w, k_norm_w, batch, seq_len):
    shift = _score_bound(q_norm_w, k_norm_w)
    return lax.cond(
        shift <= MAX_BOUNDED_SHIFT,
        lambda: _flash_bounded(q, k, vt, jnp.full((1, BOUNDED_TQ), shift, _F32), batch, seq_len),
        lambda: _flash_online(q, k, v, batch, seq_len))


def _mix_kernel(x_ref, oa_ref, ga_ref, hc_ref, hp_ref, hn_ref, cb_ref, gc_ref, os_ref, gs_ref,
                cw_ref, bw_ref, w_ref, fw_ref, out_ref, mixed_scr, *, tiles_per_seq, final_norm):
    i = pl.program_id(0)
    tm = x_ref.shape[0]
    f32 = lambda ref: ref[...].astype(_F32)
    mixed_scr[:, :ATTN_W] = (_rms(f32(oa_ref)) * bw_ref[:, :ATTN_W] * f32(ga_ref)).astype(_BF16)

    hc = hc_ref[...]
    t = i % tiles_per_seq
    prev_row = jnp.where(t == 0, 0.0, hp_ref[SUBLANES - 1:SUBLANES, :])
    next_row = jnp.where(t == tiles_per_seq - 1, 0.0, hn_ref[0:1, :])
    ridx = lax.broadcasted_iota(jnp.int32, hc.shape, 0)
    up = jnp.where(ridx == 0, prev_row, pltpu.roll(hc, 1, 0))
    down = jnp.where(ridx == tm - 1, next_row, pltpu.roll(hc, tm - 1, 0))
    y = f32(cb_ref) * (up * cw_ref[0:1, :] + hc * cw_ref[1:2, :] + down * cw_ref[2:3, :])
    c0 = ATTN_W
    mixed_scr[:, c0:c0 + CONV_W] = (_rms(y) * bw_ref[:, c0:c0 + CONV_W] * f32(gc_ref)).astype(_BF16)
    c1 = ATTN_W + CONV_W
    mixed_scr[:, c1:] = (_rms(f32(os_ref)) * bw_ref[:, c1:] * f32(gs_ref)).astype(_BF16)

    out = x_ref[...] + jnp.dot(mixed_scr[...], w_ref[...], preferred_element_type=_F32)
    if final_norm:
        out = _rms(out) * fw_ref[...]
    out_ref[...] = out


def _mix(xf, oa, ga, hc, cb, gc, osg, gs, cw_t, bw, w_out, layer, fw, seq_len, final_norm):
    n = xf.shape[0]
    tm = MIX_TM
    tiles_per_seq = seq_len // tm
    halo_blocks = tm // SUBLANES
    n_halo = n // SUBLANES
    row = lambda i: (i, 0)
    const = lambda i: (0, 0)
    prev = lambda i: (jnp.maximum(i * halo_blocks - 1, 0), 0)
    nxt = lambda i: (jnp.minimum((i + 1) * halo_blocks, n_halo - 1), 0)
    kern = functools.partial(_mix_kernel, tiles_per_seq=tiles_per_seq, final_norm=final_norm)
    return pl.pallas_call(
        kern,
        grid=(n // tm,),
        in_specs=[
            pl.BlockSpec((tm, D_MODEL), row),
            pl.BlockSpec((tm, ATTN_W), row),
            pl.BlockSpec((tm, ATTN_W), row),
            pl.BlockSpec((tm, CONV_W), row),
            pl.BlockSpec((SUBLANES, CONV_W), prev),
            pl.BlockSpec((SUBLANES, CONV_W), nxt),
            pl.BlockSpec((tm, CONV_W), row),
            pl.BlockSpec((tm, CONV_W), row),
            pl.BlockSpec((tm, SGU_W), row),
            pl.BlockSpec((tm, SGU_W), row),
            pl.BlockSpec((3, CONV_W), const),
            pl.BlockSpec((1, MIX_W), const),
            pl.BlockSpec((None, MIX_W, D_MODEL), lambda i: (layer, 0, 0), pipeline_mode=pl.Buffered(1)),
            pl.BlockSpec((1, D_MODEL), const),
        ],
        out_specs=pl.BlockSpec((tm, D_MODEL), row),
        out_shape=jax.ShapeDtypeStruct((n, D_MODEL), _F32),
        scratch_shapes=[pltpu.VMEM((tm, MIX_W), _BF16)],
        compiler_params=pltpu.CompilerParams(
            dimension_semantics=("parallel",), vmem_limit_bytes=VMEM_LIMIT_BYTES),
        name="mix_out",
    )(xf, oa, ga, hc, hc, hc, cb, gc, osg, gs, cw_t, bw, w_out, fw)


def kernel(x, norm_w, w_in, q_norm_w, k_norm_w, conv_w, sgu_norm_w, sgu_w, sgu_b, branch_norm_w, w_out,
           final_norm_w):
    b, s, d = x.shape
    depth = w_in.shape[0]
    n = b * s
    rope_row, rope_col = _rope_tables(s)
    sw = sgu_w.astype(_BF16)
    xf = x.reshape(n, d)
    for l in range(depth):
        q, k, v, vt, ga, h = _attn_proj(xf, norm_w[l].reshape(1, d), w_in, l, q_norm_w[l].reshape(1, HEAD_DIM),
                                        k_norm_w[l].reshape(1, HEAD_DIM), rope_row, rope_col, s)
        hc, cb, gc, osg, gs = _rest_proj(h, w_in, l, sgu_norm_w[l].reshape(1, SGU_W), sw[l], sgu_b[l].T)
        oa = _attention(q, k, v, vt, q_norm_w[l], k_norm_w[l], b, s)
        xf = _mix(xf, oa, ga, hc, cb, gc, osg, gs, conv_w[l].T, branch_norm_w[l].reshape(1, MIX_W),
                  w_out, l, final_norm_w.reshape(1, d), s, final_norm=(l == depth - 1))
    return xf.reshape(b, s, d)
```

```python
import functools

import jax
import jax.numpy as jnp
import numpy as np
from jax import lax
from jax.experimental import pallas as pl
from jax.experimental.pallas import tpu as pltpu

D_MODEL = 2048
HEAD_DIM = 128
N_Q_HEADS = 8
N_KV_HEADS = 2
Q_PER_KV = N_Q_HEADS // N_KV_HEADS
ATTN_W = N_Q_HEADS * HEAD_DIM
KV_W = N_KV_HEADS * HEAD_DIM
CONV_W = 512
SGU_GROUPS = 4
SGU_GC = 128
SGU_W = SGU_GROUPS * SGU_GC
CHUNK = 128
GRID_W = 64
ROPE_THETA = 10000.0
AXIS_ROT = HEAD_DIM // 2
EPS = 1e-6
MIX_W = ATTN_W + CONV_W + SGU_W
ATTN_PROJ_W = 2 * ATTN_W + 2 * KV_W
REST_BLOCK_W = 512
REST_BLOCKS = 7
assert CONV_W == SGU_W == REST_BLOCK_W and ATTN_PROJ_W % REST_BLOCK_W == 0
QK_SCALE_LOG2 = float(HEAD_DIM ** -0.5 * np.log2(np.e))

LANES = 128
SUBLANES = 8
VMEM_LIMIT_BYTES = 56 * 1024 * 1024

PROJ_TM = 512
MIX_TM = 512
FLASH_TQ = 1024
BOUNDED_TK = 2048
ONLINE_TK = 512
SOFTMAX_ROWS = 64
assert Q_PER_KV % 2 == 0

MAX_BOUNDED_SHIFT = 48.0

_BF16 = jnp.bfloat16
_F32 = jnp.float32


def _rms(t):
    return t * lax.rsqrt(jnp.mean(t * t, axis=-1, keepdims=True) + EPS)


def _silu(t):
    return t * (1.0 / (1.0 + jnp.exp(-t)))


def _gelu(t):
    return 0.5 * t * (1.0 + lax.erf(t * np.float32(np.sqrt(0.5))))


def _rope_tables(seq_len):
    inv_freq = ROPE_THETA ** (-jnp.arange(0, AXIS_ROT, 2, dtype=_F32) / AXIS_ROT)
    first_half = jnp.arange(AXIS_ROT) < (AXIS_ROT // 2)

    def axis_table(count):
        ang = jnp.arange(count, dtype=_F32)[:, None] * inv_freq
        ang = jnp.concatenate([ang, ang], axis=1)
        cos, sin = jnp.cos(ang), jnp.sin(ang)
        return jnp.stack([cos, jnp.where(first_half, -sin, 0.0), jnp.where(first_half, 0.0, sin)])

    row_t, col_t = axis_table(seq_len // GRID_W), axis_table(GRID_W)
    rope_row = jnp.concatenate([row_t, jnp.zeros_like(row_t)], axis=-1)
    rope_col = jnp.concatenate([jnp.zeros_like(col_t), col_t], axis=-1)
    return rope_row, rope_col


def _attn_proj_kernel(x_ref, nw_ref, w_ref, qw_ref, kw_ref, rrow_ref, rcol_ref,
                      q_ref, k_ref, v_ref, vt_ref, g_ref, h_ref):
    tm = x_ref.shape[0]
    half = tm // 2
    grid_rows = half // GRID_W
    qw, kw = qw_ref[...], kw_ref[...]
    pair = 2 * HEAD_DIM
    g0 = ATTN_W + 2 * KV_W

    def rope_table(which, hf):
        return jnp.concatenate(
            [rrow_ref[which, hf * grid_rows + g:hf * grid_rows + g + 1, :] + rcol_ref[which]
             for g in range(grid_rows)], axis=0)

    for hf in range(2):
        rows = slice(hf * half, (hf + 1) * half)
        h_ref[rows, :] = (_rms(x_ref[rows, :]) * nw_ref[...]).astype(_BF16)
        cos, s_lo, s_hi = rope_table(0, hf), rope_table(1, hf), rope_table(2, hf)

        def norm_rope(t, w):
            tn = _rms(t) * w
            return (tn * cos + pltpu.roll(tn, HEAD_DIM - AXIS_ROT // 2, 1) * s_lo
                    + pltpu.roll(tn, AXIS_ROT // 2, 1) * s_hi)

        def proj(lo, width):
            return jnp.dot(h_ref[rows, :], w_ref[:, lo:lo + width], preferred_element_type=_F32)

        for c in range(ATTN_W // pair):
            acc = proj(c * pair, pair)
            for hh in range(2):
                t = norm_rope(acc[:, hh * HEAD_DIM:(hh + 1) * HEAD_DIM], qw) * QK_SCALE_LOG2
                q_ref[2 * c + hh, rows, :] = t.astype(_BF16)
        acc = proj(ATTN_W, KV_W)
        for hh in range(N_KV_HEADS):
            t = norm_rope(acc[:, hh * HEAD_DIM:(hh + 1) * HEAD_DIM], kw)
            k_ref[rows, hh * HEAD_DIM:(hh + 1) * HEAD_DIM] = t.astype(_BF16)
        acc = proj(ATTN_W + KV_W, KV_W)
        v_ref[rows, :] = acc.astype(_BF16)
        vt_ref[:, rows] = acc.T.astype(_BF16)
        for c in range(ATTN_W // 512):
            g_ref[rows, c * 512:(c + 1) * 512] = _silu(proj(g0 + c * 512, 512)).astype(_BF16)


def _attn_proj(xf, nw, w_in, layer, qw, kw, rope_row, rope_col, seq_len):
    n = xf.shape[0]
    tm = PROJ_TM
    assert tm // GRID_W == SUBLANES
    tiles_per_seq = seq_len // tm
    row = lambda i: (i, 0)
    const = lambda i: (0, 0)
    return pl.pallas_call(
        _attn_proj_kernel,
        grid=(n // tm,),
        in_specs=[
            pl.BlockSpec((tm, D_MODEL), row),
            pl.BlockSpec((1, D_MODEL), const),
            pl.BlockSpec((None, D_MODEL, ATTN_PROJ_W), lambda i: (layer, 0, 0), pipeline_mode=pl.Buffered(1)),
            pl.BlockSpec((1, HEAD_DIM), const),
            pl.BlockSpec((1, HEAD_DIM), const),
            pl.BlockSpec((3, SUBLANES, HEAD_DIM), lambda i: (0, i % tiles_per_seq, 0)),
            pl.BlockSpec((3, GRID_W, HEAD_DIM), lambda i: (0, 0, 0)),
        ],
        out_specs=[
            pl.BlockSpec((N_Q_HEADS, tm, HEAD_DIM), lambda i: (0, i, 0)),
            pl.BlockSpec((tm, KV_W), row),
            pl.BlockSpec((tm, KV_W), row),
            pl.BlockSpec((KV_W, tm), lambda i: (0, i)),
            pl.BlockSpec((tm, ATTN_W), row),
            pl.BlockSpec((tm, D_MODEL), row),
        ],
        out_shape=[
            jax.ShapeDtypeStruct((N_Q_HEADS, n, HEAD_DIM), _BF16),
            jax.ShapeDtypeStruct((n, KV_W), _BF16),
            jax.ShapeDtypeStruct((n, KV_W), _BF16),
            jax.ShapeDtypeStruct((KV_W, n), _BF16),
            jax.ShapeDtypeStruct((n, ATTN_W), _BF16),
            jax.ShapeDtypeStruct((n, D_MODEL), _BF16),
        ],
        compiler_params=pltpu.CompilerParams(
            dimension_semantics=("parallel",), vmem_limit_bytes=VMEM_LIMIT_BYTES),
        name="attn_proj",
    )(xf, nw, w_in, qw, kw, rope_row, rope_col)


def _rest_proj_kernel(h_ref, *refs):
    w_refs = refs[:REST_BLOCKS]
    snw_ref, sw_ref, sb_ref, hc_ref, cb_ref, gc_ref, os_ref, gs_ref = refs[REST_BLOCKS:]
    tm = h_ref.shape[0]

    def proj(idx):
        return jnp.dot(h_ref[...], w_refs[idx][...], preferred_element_type=_F32)

    u = _gelu(proj(4))
    v = _gelu(proj(5))
    hc_ref[...] = proj(2) * proj(0)
    n_chunks = tm // CHUNK
    for g in range(SGU_GROUPS):
        lo = g * SGU_GC
        vn = (_rms(v[:, lo:lo + SGU_GC]) * snw_ref[:, lo:lo + SGU_GC]).astype(_BF16)
        rhs = jnp.concatenate([vn[c * CHUNK:(c + 1) * CHUNK, :] for c in range(n_chunks)], axis=1)
        s = jnp.dot(sw_ref[g], rhs, preferred_element_type=_F32) + sb_ref[:, g:g + 1]
        for c in range(n_chunks):
            os_ref[c * CHUNK:(c + 1) * CHUNK, lo:lo + SGU_GC] = (
                u[c * CHUNK:(c + 1) * CHUNK, lo:lo + SGU_GC] * s[:, c * SGU_GC:(c + 1) * SGU_GC]
            ).astype(_BF16)
    gc_ref[...] = _silu(proj(3)).astype(_BF16)
    gs_ref[...] = _silu(proj(6)).astype(_BF16)
    cb_ref[...] = proj(1).astype(_BF16)


def _rest_proj(h, w_in, layer, snw, sw, sb_t):
    n = h.shape[0]
    tm = PROJ_TM
    row = lambda i: (i, 0)
    const = lambda i: (0, 0)
    outs = [jax.ShapeDtypeStruct((n, REST_BLOCK_W), dt) for dt in (_F32, _BF16, _BF16, _BF16, _BF16)]
    first_block = ATTN_PROJ_W // REST_BLOCK_W
    w_specs = [
        pl.BlockSpec((None, D_MODEL, REST_BLOCK_W), (lambda i, j=j: (layer, 0, first_block + j)),
                     pipeline_mode=pl.Buffered(1))
        for j in range(REST_BLOCKS)
    ]
    return pl.pallas_call(
        _rest_proj_kernel,
        grid=(n // tm,),
        in_specs=[
            pl.BlockSpec((tm, D_MODEL), row),
            *w_specs,
            pl.BlockSpec((1, SGU_W), const),
            pl.BlockSpec((SGU_GROUPS, CHUNK, CHUNK), lambda i: (0, 0, 0)),
            pl.BlockSpec((CHUNK, SGU_GROUPS), const),
        ],
        out_specs=[pl.BlockSpec((tm, REST_BLOCK_W), row)] * 5,
        out_shape=outs,
        compiler_params=pltpu.CompilerParams(
            dimension_semantics=("parallel",), vmem_limit_bytes=VMEM_LIMIT_BYTES),
        name="rest_proj",
    )(h, *([w_in] * REST_BLOCKS), snw, sw, sb_t)


def _flash_online_kernel(q_ref, k_ref, v_ref, o_ref, s_scr, p_scr, m_scr, l_scr, acc_scr, *, tk, rb):
    tq = q_ref.shape[1]
    n_chunks = k_ref.shape[0] // tk

    m_scr[...] = jnp.full(m_scr.shape, -jnp.inf, _F32)
    l_scr[...] = jnp.zeros(l_scr.shape, _F32)
    acc_scr[...] = jnp.zeros(acc_scr.shape, _F32)
    p_scr[1] = jnp.zeros(p_scr.shape[1:], _BF16)

    def kv_rows(c):
        return pl.ds(pl.multiple_of(c * tk, tk), tk)

    def qk(g, c):
        return lax.dot_general(q_ref[g], k_ref[kv_rows(c), :], (((1,), (1,)), ((), ())),
                               preferred_element_type=_F32)

    def pv(g, slot, c):
        acc_scr[g] += jnp.dot(p_scr[slot], v_ref[kv_rows(c), :], preferred_element_type=_F32)

    def softmax(g, slot):
        for r in range(tq // rb):
            rows = slice(r * rb, (r + 1) * rb)
            s = s_scr[slot, rows, :]
            m_prev = m_scr[g, rows, :]
            m_next = jnp.maximum(m_prev, jnp.max(s, axis=1, keepdims=True))
            alpha = jnp.exp2(m_prev - m_next)
            p = jnp.exp2(s - jnp.tile(m_next, (1, tk // LANES)))
            l_scr[g, rows, :] = alpha * l_scr[g, rows, :] + jnp.sum(p, axis=1, keepdims=True)
            m_scr[g, rows, :] = m_next
            acc_scr[g, rows, :] = alpha * acc_scr[g, rows, :]
            p_scr[slot, rows, :] = p.astype(_BF16)

    s_scr[0] = qk(0, 0)

    def body(c, carry):
        for g in range(Q_PER_KV):
            slot = g % 2
            if g < Q_PER_KV - 1:
                s_scr[1 - slot] = qk(g + 1, c)
            else:
                s_scr[1 - slot] = qk(0, jnp.minimum(c + 1, n_chunks - 1))
            softmax(g, slot)
            if g == 0:
                pv(Q_PER_KV - 1, 1, jnp.maximum(c - 1, 0))
            else:
                pv(g - 1, 1 - slot, c)
        return carry

    lax.fori_loop(0, n_chunks, body, 0)
    pv(Q_PER_KV - 1, 1, n_chunks - 1)

    for g in range(Q_PER_KV):
        o_ref[:, g * HEAD_DIM:(g + 1) * HEAD_DIM] = (acc_scr[g] / l_scr[g]).astype(_BF16)


def _flash_bounded_kernel(shift_ref, q_ref, k_ref, vt_ref, o_ref, p_scr, l_scr, acc_scr, *, tk):
    tq = q_ref.shape[1]
    n_chunks = k_ref.shape[0] // tk
    shift = shift_ref[...]

    def kv_rows(c):
        return pl.ds(pl.multiple_of(c * tk, tk), tk)

    def probs(g, c, slot, first):
        s = lax.dot_general(k_ref[kv_rows(c), :], q_ref[g], (((1,), (1,)), ((), ())),
                            preferred_element_type=_F32)
        p = jnp.exp2(s - shift)
        part = jnp.sum(p.reshape(tk // SUBLANES, SUBLANES, tq), axis=0)
        if first:
            l_scr[g] = part
        else:
            l_scr[g] += part
        p_scr[slot] = p.astype(_BF16)

    def pv(g, slot, c, first):
        d = jnp.dot(vt_ref[:, kv_rows(c)], p_scr[slot], preferred_element_type=_F32)
        if first:
            acc_scr[g] = d
        else:
            acc_scr[g] += d

    def chunk(c, first=False, last=False):
        for g in range(Q_PER_KV):
            slot = g % 2
            if g < Q_PER_KV - 1:
                probs(g + 1, c, 1 - slot, first)
            elif not last:
                probs(0, c + 1, 1 - slot, False)
            pv(g, slot, c, first)

    probs(0, 0, 0, True)
    chunk(0, first=True)

    def body(c, carry):
        chunk(c)
        return carry

    lax.fori_loop(1, n_chunks - 1, body, 0)
    chunk(n_chunks - 1, last=True)

    for g in range(Q_PER_KV):
        l = jnp.sum(l_scr[g], axis=0, keepdims=True)
        o_ref[:, g * HEAD_DIM:(g + 1) * HEAD_DIM] = (acc_scr[g] / l).T.astype(_BF16)


def _flash_kernel(use_bounded_ref, shift_ref, q_ref, k_ref, v_ref, vt_ref, o_ref,
                  pt_scr, lsub_scr, acct_scr, s_scr, p_scr, m_scr, l_scr, acc_scr):
    @pl.when(use_bounded_ref[0] == 1)
    def _():
        _flash_bounded_kernel(shift_ref, q_ref, k_ref, vt_ref, o_ref, pt_scr, lsub_scr, acct_scr, tk=BOUNDED_TK)

    @pl.when(use_bounded_ref[0] != 1)
    def _():
        _flash_online_kernel(q_ref, k_ref, v_ref, o_ref, s_scr, p_scr, m_scr, l_scr, acc_scr,
                             tk=ONLINE_TK, rb=SOFTMAX_ROWS)


def _flash(use_bounded, shift, q, k, v, vt, batch, seq_len):
    tq = FLASH_TQ
    assert seq_len // BOUNDED_TK >= 2
    q_tiles = seq_len // tq
    group_w = Q_PER_KV * HEAD_DIM
    grid_spec = pltpu.PrefetchScalarGridSpec(
        num_scalar_prefetch=1,
        grid=(batch, N_KV_HEADS, q_tiles),
        in_specs=[
            pl.BlockSpec((1, tq), lambda bi, hi, qi, flag: (0, 0)),
            pl.BlockSpec((Q_PER_KV, tq, HEAD_DIM), lambda bi, hi, qi, flag: (hi, bi * q_tiles + qi, 0)),
            pl.BlockSpec((seq_len, HEAD_DIM), lambda bi, hi, qi, flag: (bi, hi)),
            pl.BlockSpec((seq_len, HEAD_DIM), lambda bi, hi, qi, flag: (bi, hi)),
            pl.BlockSpec((HEAD_DIM, seq_len), lambda bi, hi, qi, flag: (hi, bi)),
        ],
        out_specs=pl.BlockSpec((tq, group_w), lambda bi, hi, qi, flag: (bi * q_tiles + qi, hi)),
        scratch_shapes=[
            pltpu.VMEM((2, BOUNDED_TK, tq), _BF16),
            pltpu.VMEM((Q_PER_KV, SUBLANES, tq), _F32),
            pltpu.VMEM((Q_PER_KV, HEAD_DIM, tq), _F32),
            pltpu.VMEM((2, tq, ONLINE_TK), _F32),
            pltpu.VMEM((2, tq, ONLINE_TK), _BF16),
            pltpu.VMEM((Q_PER_KV, tq, LANES), _F32),
            pltpu.VMEM((Q_PER_KV, tq, LANES), _F32),
            pltpu.VMEM((Q_PER_KV, tq, HEAD_DIM), _F32),
        ],
    )
    return pl.pallas_call(
        _flash_kernel,
        grid_spec=grid_spec,
        out_shape=jax.ShapeDtypeStruct((batch * seq_len, ATTN_W), _BF16),
        compiler_params=pltpu.CompilerParams(
            dimension_semantics=("parallel", "parallel", "parallel"),
            vmem_limit_bytes=VMEM_LIMIT_BYTES),
        name="flash_attn",
    )(use_bounded, shift, q, k, v, vt)


def _score_bound(q_norm_w, k_norm_w):
    rounding = (1.0 + 2.0 ** -8) ** 2
    return (HEAD_DIM * QK_SCALE_LOG2 * rounding) * jnp.max(jnp.abs(q_norm_w)) * jnp.max(jnp.abs(k_norm_w))


def _attention(q, k, v, vt, q_norm_w, k_norm_w, batch, seq_len):
    shift = _score_bound(q_norm_w, k_norm_w)
    use_bounded = (shift <= MAX_BOUNDED_SHIFT).astype(jnp.int32).reshape(1)
    return _flash(use_bounded, jnp.full((1, FLASH_TQ), shift, _F32), q, k, v, vt, batch, seq_len)


def _mix_kernel(x_ref, oa_ref, ga_ref, hc_ref, hp_ref, hn_ref, cb_ref, gc_ref, os_ref, gs_ref,
                cw_ref, bw_ref, w_ref, fw_ref, out_ref, mixed_scr, *, tiles_per_seq, final_norm):
    i = pl.program_id(0)
    tm = x_ref.shape[0]
    f32 = lambda ref: ref[...].astype(_F32)
    mixed_scr[:, :ATTN_W] = (_rms(f32(oa_ref)) * bw_ref[:, :ATTN_W] * f32(ga_ref)).astype(_BF16)

    hc = hc_ref[...]
    t = i % tiles_per_seq
    prev_row = jnp.where(t == 0, 0.0, hp_ref[SUBLANES - 1:SUBLANES, :])
    next_row = jnp.where(t == tiles_per_seq - 1, 0.0, hn_ref[0:1, :])
    ridx = lax.broadcasted_iota(jnp.int32, hc.shape, 0)
    up = jnp.where(ridx == 0, prev_row, pltpu.roll(hc, 1, 0))
    down = jnp.where(ridx == tm - 1, next_row, pltpu.roll(hc, tm - 1, 0))
    y = f32(cb_ref) * (up * cw_ref[0:1, :] + hc * cw_ref[1:2, :] + down * cw_ref[2:3, :])
    c0 = ATTN_W
    mixed_scr[:, c0:c0 + CONV_W] = (_rms(y) * bw_ref[:, c0:c0 + CONV_W] * f32(gc_ref)).astype(_BF16)
    c1 = ATTN_W + CONV_W
    mixed_scr[:, c1:] = (_rms(f32(os_ref)) * bw_ref[:, c1:] * f32(gs_ref)).astype(_BF16)

    out = x_ref[...] + jnp.dot(mixed_scr[...], w_ref[...], preferred_element_type=_F32)
    if final_norm:
        out = _rms(out) * fw_ref[...]
    out_ref[...] = out


def _mix(xf, oa, ga, hc, cb, gc, osg, gs, cw_t, bw, w_out, layer, fw, seq_len, final_norm):
    n = xf.shape[0]
    tm = MIX_TM
    tiles_per_seq = seq_len // tm
    halo_blocks = tm // SUBLANES
    n_halo = n // SUBLANES
    row = lambda i: (i, 0)
    const = lambda i: (0, 0)
    prev = lambda i: (jnp.maximum(i * halo_blocks - 1, 0), 0)
    nxt = lambda i: (jnp.minimum((i + 1) * halo_blocks, n_halo - 1), 0)
    kern = functools.partial(_mix_kernel, tiles_per_seq=tiles_per_seq, final_norm=final_norm)
    return pl.pallas_call(
        kern,
        grid=(n // tm,),
        in_specs=[
            pl.BlockSpec((tm, D_MODEL), row),
            pl.BlockSpec((tm, ATTN_W), row),
            pl.BlockSpec((tm, ATTN_W), row),
            pl.BlockSpec((tm, CONV_W), row),
            pl.BlockSpec((SUBLANES, CONV_W), prev),
            pl.BlockSpec((SUBLANES, CONV_W), nxt),
            pl.BlockSpec((tm, CONV_W), row),
            pl.BlockSpec((tm, CONV_W), row),
            pl.BlockSpec((tm, SGU_W), row),
            pl.BlockSpec((tm, SGU_W), row),
            pl.BlockSpec((3, CONV_W), const),
            pl.BlockSpec((1, MIX_W), const),
            pl.BlockSpec((None, MIX_W, D_MODEL), lambda i: (layer, 0, 0), pipeline_mode=pl.Buffered(1)),
            pl.BlockSpec((1, D_MODEL), const),
        ],
        out_specs=pl.BlockSpec((tm, D_MODEL), row),
        out_shape=jax.ShapeDtypeStruct((n, D_MODEL), _F32),
        scratch_shapes=[pltpu.VMEM((tm, MIX_W), _BF16)],
        compiler_params=pltpu.CompilerParams(
            dimension_semantics=("parallel",), vmem_limit_bytes=VMEM_LIMIT_BYTES),
        name="mix_out",
    )(xf, oa, ga, hc, hc, hc, cb, gc, osg, gs, cw_t, bw, w_out, fw)


def kernel(x, norm_w, w_in, q_norm_w, k_norm_w, conv_w, sgu_norm_w, sgu_w, sgu_b, branch_norm_w, w_out,
           final_norm_w):
    b, s, d = x.shape
    depth = w_in.shape[0]
    n = b * s
    rope_row, rope_col = _rope_tables(s)
    sw = sgu_w.astype(_BF16)
    xf = x.reshape(n, d)
    for l in range(depth):
        q, k, v, vt, ga, h = _attn_proj(xf, norm_w[l].reshape(1, d), w_in, l, q_norm_w[l].reshape(1, HEAD_DIM),
                                        k_norm_w[l].reshape(1, HEAD_DIM), rope_row, rope_col, s)
        hc, cb, gc, osg, gs = _rest_proj(h, w_in, l, sgu_norm_w[l].reshape(1, SGU_W), sw[l], sgu_b[l].T)
        oa = _attention(q, k, v, vt, q_norm_w[l], k_norm_w[l], b, s)
        xf = _mix(xf, oa, ga, hc, cb, gc, osg, gs, conv_w[l].T, branch_norm_w[l].reshape(1, MIX_W),
                  w_out, l, final_norm_w.reshape(1, d), s, final_norm=(l == depth - 1))
    return xf.reshape(b, s, d)
```

```python
import functools

import jax
import jax.numpy as jnp
import numpy as np
from jax import lax
from jax.experimental import pallas as pl
from jax.experimental.pallas import tpu as pltpu

D_MODEL = 2048
HEAD_DIM = 128
N_Q_HEADS = 8
N_KV_HEADS = 2
Q_PER_KV = N_Q_HEADS // N_KV_HEADS
ATTN_W = N_Q_HEADS * HEAD_DIM
KV_W = N_KV_HEADS * HEAD_DIM
CONV_W = 512
SGU_GROUPS = 4
SGU_GC = 128
SGU_W = SGU_GROUPS * SGU_GC
CHUNK = 128
GRID_W = 64
ROPE_THETA = 10000.0
AXIS_ROT = HEAD_DIM // 2
EPS = 1e-6
MIX_W = ATTN_W + CONV_W + SGU_W
ATTN_PROJ_W = 2 * ATTN_W + 2 * KV_W
REST_BLOCK_W = 512
REST_BLOCKS = 7
assert CONV_W == SGU_W == REST_BLOCK_W and ATTN_PROJ_W % REST_BLOCK_W == 0
QK_SCALE_LOG2 = float(HEAD_DIM ** -0.5 * np.log2(np.e))

LANES = 128
SUBLANES = 8
VMEM_LIMIT_BYTES = 56 * 1024 * 1024

PROJ_TM = 512
MIX_TM = 512
BOUNDED_TQ = 1024
ONLINE_TQ = 512
BOUNDED_TK = 2048
ONLINE_TK = 512
SOFTMAX_ROWS = 64
assert Q_PER_KV % 2 == 0

MAX_BOUNDED_SHIFT = 48.0

_BF16 = jnp.bfloat16
_F32 = jnp.float32


def _rms(t):
    return t * lax.rsqrt(jnp.mean(t * t, axis=-1, keepdims=True) + EPS)


def _silu(t):
    return t * (1.0 / (1.0 + jnp.exp(-t)))


def _gelu(t):
    return 0.5 * t * (1.0 + lax.erf(t * np.float32(np.sqrt(0.5))))


def _rope_tables(seq_len):
    inv_freq = ROPE_THETA ** (-jnp.arange(0, AXIS_ROT, 2, dtype=_F32) / AXIS_ROT)
    first_half = jnp.arange(AXIS_ROT) < (AXIS_ROT // 2)

    def axis_table(count):
        ang = jnp.arange(count, dtype=_F32)[:, None] * inv_freq
        ang = jnp.concatenate([ang, ang], axis=1)
        cos, sin = jnp.cos(ang), jnp.sin(ang)
        return jnp.stack([cos, jnp.where(first_half, -sin, 0.0), jnp.where(first_half, 0.0, sin)])

    row_t, col_t = axis_table(seq_len // GRID_W), axis_table(GRID_W)
    rope_row = jnp.concatenate([row_t, jnp.zeros_like(row_t)], axis=-1)
    rope_col = jnp.concatenate([jnp.zeros_like(col_t), col_t], axis=-1)
    return rope_row, rope_col


def _attn_proj_kernel(x_ref, nw_ref, w_ref, qw_ref, kw_ref, rrow_ref, rcol_ref,
                      q_ref, k_ref, v_ref, vt_ref, g_ref, h_ref, wbf_scr):
    tm = x_ref.shape[0]

    @pl.when(pl.program_id(0) == 0)
    def _():
        for c in range(ATTN_PROJ_W // 512):
            wbf_scr[:, c * 512:(c + 1) * 512] = w_ref[:, c * 512:(c + 1) * 512].astype(_BF16)

    half = tm // 2
    grid_rows = half // GRID_W
    qw, kw = qw_ref[...], kw_ref[...]
    pair = 2 * HEAD_DIM
    g0 = ATTN_W + 2 * KV_W

    def rope_table(which, hf):
        return jnp.concatenate(
            [rrow_ref[which, hf * grid_rows + g:hf * grid_rows + g + 1, :] + rcol_ref[which]
             for g in range(grid_rows)], axis=0)

    for hf in range(2):
        rows = slice(hf * half, (hf + 1) * half)
        h_ref[rows, :] = (_rms(x_ref[rows, :]) * nw_ref[...]).astype(_BF16)
        cos, s_lo, s_hi = rope_table(0, hf), rope_table(1, hf), rope_table(2, hf)

        def norm_rope(t, w):
            tn = _rms(t) * w
            return (tn * cos + pltpu.roll(tn, HEAD_DIM - AXIS_ROT // 2, 1) * s_lo
                    + pltpu.roll(tn, AXIS_ROT // 2, 1) * s_hi)

        def proj(lo, width):
            return jnp.dot(h_ref[rows, :], wbf_scr[:, lo:lo + width], preferred_element_type=_F32)

        for c in range(ATTN_W // pair):
            acc = proj(c * pair, pair)
            for hh in range(2):
                t = norm_rope(acc[:, hh * HEAD_DIM:(hh + 1) * HEAD_DIM], qw) * QK_SCALE_LOG2
                q_ref[2 * c + hh, rows, :] = t.astype(_BF16)
        acc = proj(ATTN_W, KV_W)
        for hh in range(N_KV_HEADS):
            t = norm_rope(acc[:, hh * HEAD_DIM:(hh + 1) * HEAD_DIM], kw)
            k_ref[rows, hh * HEAD_DIM:(hh + 1) * HEAD_DIM] = t.astype(_BF16)
        acc = proj(ATTN_W + KV_W, KV_W)
        v_ref[rows, :] = acc.astype(_BF16)
        vt_ref[:, rows] = acc.T.astype(_BF16)
        for c in range(ATTN_W // 512):
            g_ref[rows, c * 512:(c + 1) * 512] = _silu(proj(g0 + c * 512, 512)).astype(_BF16)


def _attn_proj(xf, nw, w_in, layer, qw, kw, rope_row, rope_col, seq_len):
    n = xf.shape[0]
    tm = PROJ_TM
    assert tm // GRID_W == SUBLANES
    tiles_per_seq = seq_len // tm
    row = lambda i: (i, 0)
    const = lambda i: (0, 0)
    return pl.pallas_call(
        _attn_proj_kernel,
        grid=(n // tm,),
        in_specs=[
            pl.BlockSpec((tm, D_MODEL), row),
            pl.BlockSpec((1, D_MODEL), const),
            pl.BlockSpec((None, D_MODEL, ATTN_PROJ_W), lambda i: (layer, 0, 0), pipeline_mode=pl.Buffered(1)),
            pl.BlockSpec((1, HEAD_DIM), const),
            pl.BlockSpec((1, HEAD_DIM), const),
            pl.BlockSpec((3, SUBLANES, HEAD_DIM), lambda i: (0, i % tiles_per_seq, 0)),
            pl.BlockSpec((3, GRID_W, HEAD_DIM), lambda i: (0, 0, 0)),
        ],
        out_specs=[
            pl.BlockSpec((N_Q_HEADS, tm, HEAD_DIM), lambda i: (0, i, 0)),
            pl.BlockSpec((tm, KV_W), row),
            pl.BlockSpec((tm, KV_W), row),
            pl.BlockSpec((KV_W, tm), lambda i: (0, i)),
            pl.BlockSpec((tm, ATTN_W), row),
            pl.BlockSpec((tm, D_MODEL), row),
        ],
        out_shape=[
            jax.ShapeDtypeStruct((N_Q_HEADS, n, HEAD_DIM), _BF16),
            jax.ShapeDtypeStruct((n, KV_W), _BF16),
            jax.ShapeDtypeStruct((n, KV_W), _BF16),
            jax.ShapeDtypeStruct((KV_W, n), _BF16),
            jax.ShapeDtypeStruct((n, ATTN_W), _BF16),
            jax.ShapeDtypeStruct((n, D_MODEL), _BF16),
        ],
        scratch_shapes=[pltpu.VMEM((D_MODEL, ATTN_PROJ_W), _BF16)],
        compiler_params=pltpu.CompilerParams(
            dimension_semantics=("arbitrary",), vmem_limit_bytes=VMEM_LIMIT_BYTES),
        name="attn_proj",
    )(xf, nw, w_in, qw, kw, rope_row, rope_col)


def _rest_proj_kernel(h_ref, *refs):
    w_refs = refs[:REST_BLOCKS]
    snw_ref, sw_ref, sb_ref, hc_ref, cb_ref, gc_ref, os_ref, gs_ref = refs[REST_BLOCKS:]
    tm = h_ref.shape[0]

    def proj(idx):
        return jnp.dot(h_ref[...], w_refs[idx][...], preferred_element_type=_F32)

    u = _gelu(proj(4))
    v = _gelu(proj(5))
    hc_ref[...] = proj(2) * proj(0)
    n_chunks = tm // CHUNK
    for g in range(SGU_GROUPS):
        lo = g * SGU_GC
        vn = (_rms(v[:, lo:lo + SGU_GC]) * snw_ref[:, lo:lo + SGU_GC]).astype(_BF16)
        rhs = jnp.concatenate([vn[c * CHUNK:(c + 1) * CHUNK, :] for c in range(n_chunks)], axis=1)
        s = jnp.dot(sw_ref[g], rhs, preferred_element_type=_F32) + sb_ref[:, g:g + 1]
        for c in range(n_chunks):
            os_ref[c * CHUNK:(c + 1) * CHUNK, lo:lo + SGU_GC] = (
                u[c * CHUNK:(c + 1) * CHUNK, lo:lo + SGU_GC] * s[:, c * SGU_GC:(c + 1) * SGU_GC]
            ).astype(_BF16)
    gc_ref[...] = _silu(proj(3)).astype(_BF16)
    gs_ref[...] = _silu(proj(6)).astype(_BF16)
    cb_ref[...] = proj(1).astype(_BF16)


def _rest_proj(h, w_in, layer, snw, sw, sb_t):
    n = h.shape[0]
    tm = PROJ_TM
    row = lambda i: (i, 0)
    const = lambda i: (0, 0)
    outs = [jax.ShapeDtypeStruct((n, REST_BLOCK_W), dt) for dt in (_F32, _BF16, _BF16, _BF16, _BF16)]
    first_block = ATTN_PROJ_W // REST_BLOCK_W
    w_specs = [
        pl.BlockSpec((None, D_MODEL, REST_BLOCK_W), (lambda i, j=j: (layer, 0, first_block + j)),
                     pipeline_mode=pl.Buffered(1))
        for j in range(REST_BLOCKS)
    ]
    return pl.pallas_call(
        _rest_proj_kernel,
        grid=(n // tm,),
        in_specs=[
            pl.BlockSpec((tm, D_MODEL), row),
            *w_specs,
            pl.BlockSpec((1, SGU_W), const),
            pl.BlockSpec((SGU_GROUPS, CHUNK, CHUNK), lambda i: (0, 0, 0)),
            pl.BlockSpec((CHUNK, SGU_GROUPS), const),
        ],
        out_specs=[pl.BlockSpec((tm, REST_BLOCK_W), row)] * 5,
        out_shape=outs,
        compiler_params=pltpu.CompilerParams(
            dimension_semantics=("parallel",), vmem_limit_bytes=VMEM_LIMIT_BYTES),
        name="rest_proj",
    )(h, *([w_in] * REST_BLOCKS), snw, sw, sb_t)


def _flash_online_kernel(q_ref, k_ref, v_ref, o_ref, s_scr, p_scr, m_scr, l_scr, acc_scr, *, tk, rb):
    tq = q_ref.shape[1]
    n_chunks = k_ref.shape[0] // tk

    m_scr[...] = jnp.full(m_scr.shape, -jnp.inf, _F32)
    l_scr[...] = jnp.zeros(l_scr.shape, _F32)
    acc_scr[...] = jnp.zeros(acc_scr.shape, _F32)
    p_scr[1] = jnp.zeros(p_scr.shape[1:], _BF16)

    def kv_rows(c):
        return pl.ds(pl.multiple_of(c * tk, tk), tk)

    def qk(g, c):
        return lax.dot_general(q_ref[g], k_ref[kv_rows(c), :], (((1,), (1,)), ((), ())),
                               preferred_element_type=_F32)

    def pv(g, slot, c):
        acc_scr[g] += jnp.dot(p_scr[slot], v_ref[kv_rows(c), :], preferred_element_type=_F32)

    def softmax(g, slot):
        for r in range(tq // rb):
            rows = slice(r * rb, (r + 1) * rb)
            s = s_scr[slot, rows, :]
            m_prev = m_scr[g, rows, :]
            m_next = jnp.maximum(m_prev, jnp.max(s, axis=1, keepdims=True))
            alpha = jnp.exp2(m_prev - m_next)
            p = jnp.exp2(s - jnp.tile(m_next, (1, tk // LANES)))
            l_scr[g, rows, :] = alpha * l_scr[g, rows, :] + jnp.sum(p, axis=1, keepdims=True)
            m_scr[g, rows, :] = m_next
            acc_scr[g, rows, :] = alpha * acc_scr[g, rows, :]
            p_scr[slot, rows, :] = p.astype(_BF16)

    s_scr[0] = qk(0, 0)

    def body(c, carry):
        for g in range(Q_PER_KV):
            slot = g % 2
            if g < Q_PER_KV - 1:
                s_scr[1 - slot] = qk(g + 1, c)
            else:
                s_scr[1 - slot] = qk(0, jnp.minimum(c + 1, n_chunks - 1))
            softmax(g, slot)
            if g == 0:
                pv(Q_PER_KV - 1, 1, jnp.maximum(c - 1, 0))
            else:
                pv(g - 1, 1 - slot, c)
        return carry

    lax.fori_loop(0, n_chunks, body, 0)
    pv(Q_PER_KV - 1, 1, n_chunks - 1)

    for g in range(Q_PER_KV):
        o_ref[:, g * HEAD_DIM:(g + 1) * HEAD_DIM] = (acc_scr[g] / l_scr[g]).astype(_BF16)


def _flash_online(q, k, v, batch, seq_len):
    tq, tk = ONLINE_TQ, ONLINE_TK
    q_tiles = seq_len // tq
    group_w = Q_PER_KV * HEAD_DIM
    kern = functools.partial(_flash_online_kernel, tk=tk, rb=SOFTMAX_ROWS)
    return pl.pallas_call(
        kern,
        grid=(batch, N_KV_HEADS, q_tiles),
        in_specs=[
            pl.BlockSpec((Q_PER_KV, tq, HEAD_DIM), lambda bi, hi, qi: (hi, bi * q_tiles + qi, 0)),
            pl.BlockSpec((seq_len, HEAD_DIM), lambda bi, hi, qi: (bi, hi)),
            pl.BlockSpec((seq_len, HEAD_DIM), lambda bi, hi, qi: (bi, hi)),
        ],
        out_specs=pl.BlockSpec((tq, group_w), lambda bi, hi, qi: (bi * q_tiles + qi, hi)),
        out_shape=jax.ShapeDtypeStruct((batch * seq_len, ATTN_W), _BF16),
        scratch_shapes=[
            pltpu.VMEM((2, tq, tk), _F32),
            pltpu.VMEM((2, tq, tk), _BF16),
            pltpu.VMEM((Q_PER_KV, tq, LANES), _F32),
            pltpu.VMEM((Q_PER_KV, tq, LANES), _F32),
            pltpu.VMEM((Q_PER_KV, tq, HEAD_DIM), _F32),
        ],
        compiler_params=pltpu.CompilerParams(
            dimension_semantics=("parallel", "parallel", "parallel"),
            vmem_limit_bytes=VMEM_LIMIT_BYTES),
        name="flash_online",
    )(q, k, v)

def _flash_bounded_kernel(shift_ref, q_ref, k_ref, vt_ref, o_ref, p_scr, l_scr, acc_scr, *, tk):
    tq = q_ref.shape[1]
    n_chunks = k_ref.shape[0] // tk
    shift = shift_ref[...]

    def kv_rows(c):
        return pl.ds(pl.multiple_of(c * tk, tk), tk)

    def probs(g, c, slot, first):
        s = lax.dot_general(k_ref[kv_rows(c), :], q_ref[g], (((1,), (1,)), ((), ())),
                            preferred_element_type=_F32)
        p = jnp.exp2(s - shift)
        part = jnp.sum(p.reshape(tk // SUBLANES, SUBLANES, tq), axis=0)
        if first:
            l_scr[g] = part
        else:
            l_scr[g] += part
        p_scr[slot] = p.astype(_BF16)

    def pv(g, slot, c, first):
        d = jnp.dot(vt_ref[:, kv_rows(c)], p_scr[slot], preferred_element_type=_F32)
        if first:
            acc_scr[g] = d
        else:
            acc_scr[g] += d

    def chunk(c, first=False, last=False):
        for g in range(Q_PER_KV):
            slot = g % 2
            if g < Q_PER_KV - 1:
                probs(g + 1, c, 1 - slot, first)
            elif not last:
                probs(0, c + 1, 1 - slot, False)
            pv(g, slot, c, first)

    probs(0, 0, 0, True)
    chunk(0, first=True)

    def body(c, carry):
        chunk(c)
        return carry

    lax.fori_loop(1, n_chunks - 1, body, 0)
    chunk(n_chunks - 1, last=True)

    for g in range(Q_PER_KV):
        l = jnp.sum(l_scr[g], axis=0, keepdims=True)
        o_ref[:, g * HEAD_DIM:(g + 1) * HEAD_DIM] = (acc_scr[g] / l).T.astype(_BF16)


def _flash_bounded(q, k, vt, shift, batch, seq_len):
    tq, tk = BOUNDED_TQ, BOUNDED_TK
    assert seq_len // tk >= 2
    q_tiles = seq_len // tq
    group_w = Q_PER_KV * HEAD_DIM
    kern = functools.partial(_flash_bounded_kernel, tk=tk)
    return pl.pallas_call(
        kern,
        grid=(batch, N_KV_HEADS, q_tiles),
        in_specs=[
            pl.BlockSpec((1, tq), lambda bi, hi, qi: (0, 0)),
            pl.BlockSpec((Q_PER_KV, tq, HEAD_DIM), lambda bi, hi, qi: (hi, bi * q_tiles + qi, 0)),
            pl.BlockSpec((seq_len, HEAD_DIM), lambda bi, hi, qi: (bi, hi)),
            pl.BlockSpec((HEAD_DIM, seq_len), lambda bi, hi, qi: (hi, bi)),
        ],
        out_specs=pl.BlockSpec((tq, group_w), lambda bi, hi, qi: (bi * q_tiles + qi, hi)),
        out_shape=jax.ShapeDtypeStruct((batch * seq_len, ATTN_W), _BF16),
        scratch_shapes=[
            pltpu.VMEM((2, tk, tq), _BF16),
            pltpu.VMEM((Q_PER_KV, SUBLANES, tq), _F32),
            pltpu.VMEM((Q_PER_KV, HEAD_DIM, tq), _F32),
        ],
        compiler_params=pltpu.CompilerParams(
            dimension_semantics=("parallel", "parallel", "parallel"),
            vmem_limit_bytes=VMEM_LIMIT_BYTES),
        name="flash_bounded",
    )(shift, q, k, vt)


def _score_bound(q_norm_w, k_norm_w):
    rounding = (1.0 + 2.0 ** -8) ** 2
    return (HEAD_DIM * QK_SCALE_LOG2 * rounding) * jnp.max(jnp.abs(q_norm_w)) * jnp.max(jnp.abs(k_norm_w))


def _attention(q, k, v, vt, q_norm_w, k_norm_w, batch, seq_len):
    shift = _score_bound(q_norm_w, k_norm_w)
    return lax.cond(
        shift <= MAX_BOUNDED_SHIFT,
        lambda: _flash_bounded(q, k, vt, jnp.full((1, BOUNDED_TQ), shift, _F32), batch, seq_len),
        lambda: _flash_online(q, k, v, batch, seq_len))


def _mix_kernel(x_ref, oa_ref, ga_ref, hc_ref, hp_ref, hn_ref, cb_ref, gc_ref, os_ref, gs_ref,
                cw_ref, bw_ref, w_ref, fw_ref, out_ref, mixed_scr, wbf_scr, *, tiles_per_seq, final_norm):
    i = pl.program_id(0)
    tm = x_ref.shape[0]

    @pl.when(i == 0)
    def _():
        for c in range(D_MODEL // 512):
            wbf_scr[:, c * 512:(c + 1) * 512] = w_ref[:, c * 512:(c + 1) * 512].astype(_BF16)

    f32 = lambda ref: ref[...].astype(_F32)
    mixed_scr[:, :ATTN_W] = (_rms(f32(oa_ref)) * bw_ref[:, :ATTN_W] * f32(ga_ref)).astype(_BF16)

    hc = hc_ref[...]
    t = i % tiles_per_seq
    prev_row = jnp.where(t == 0, 0.0, hp_ref[SUBLANES - 1:SUBLANES, :])
    next_row = jnp.where(t == tiles_per_seq - 1, 0.0, hn_ref[0:1, :])
    ridx = lax.broadcasted_iota(jnp.int32, hc.shape, 0)
    up = jnp.where(ridx == 0, prev_row, pltpu.roll(hc, 1, 0))
    down = jnp.where(ridx == tm - 1, next_row, pltpu.roll(hc, tm - 1, 0))
    y = f32(cb_ref) * (up * cw_ref[0:1, :] + hc * cw_ref[1:2, :] + down * cw_ref[2:3, :])
    c0 = ATTN_W
    mixed_scr[:, c0:c0 + CONV_W] = (_rms(y) * bw_ref[:, c0:c0 + CONV_W] * f32(gc_ref)).astype(_BF16)
    c1 = ATTN_W + CONV_W
    mixed_scr[:, c1:] = (_rms(f32(os_ref)) * bw_ref[:, c1:] * f32(gs_ref)).astype(_BF16)

    out = x_ref[...] + jnp.dot(mixed_scr[...], wbf_scr[...], preferred_element_type=_F32)
    if final_norm:
        out = _rms(out) * fw_ref[...]
    out_ref[...] = out


def _mix(xf, oa, ga, hc, cb, gc, osg, gs, cw_t, bw, w_out, layer, fw, seq_len, final_norm):
    n = xf.shape[0]
    tm = MIX_TM
    tiles_per_seq = seq_len // tm
    halo_blocks = tm // SUBLANES
    n_halo = n // SUBLANES
    row = lambda i: (i, 0)
    const = lambda i: (0, 0)
    prev = lambda i: (jnp.maximum(i * halo_blocks - 1, 0), 0)
    nxt = lambda i: (jnp.minimum((i + 1) * halo_blocks, n_halo - 1), 0)
    kern = functools.partial(_mix_kernel, tiles_per_seq=tiles_per_seq, final_norm=final_norm)
    return pl.pallas_call(
        kern,
        grid=(n // tm,),
        in_specs=[
            pl.BlockSpec((tm, D_MODEL), row),
            pl.BlockSpec((tm, ATTN_W), row),
            pl.BlockSpec((tm, ATTN_W), row),
            pl.BlockSpec((tm, CONV_W), row),
            pl.BlockSpec((SUBLANES, CONV_W), prev),
            pl.BlockSpec((SUBLANES, CONV_W), nxt),
            pl.BlockSpec((tm, CONV_W), row),
            pl.BlockSpec((tm, CONV_W), row),
            pl.BlockSpec((tm, SGU_W), row),
            pl.BlockSpec((tm, SGU_W), row),
            pl.BlockSpec((3, CONV_W), const),
            pl.BlockSpec((1, MIX_W), const),
            pl.BlockSpec((None, MIX_W, D_MODEL), lambda i: (layer, 0, 0), pipeline_mode=pl.Buffered(1)),
            pl.BlockSpec((1, D_MODEL), const),
        ],
        out_specs=pl.BlockSpec((tm, D_MODEL), row),
        out_shape=jax.ShapeDtypeStruct((n, D_MODEL), _F32),
        scratch_shapes=[pltpu.VMEM((tm, MIX_W), _BF16), pltpu.VMEM((MIX_W, D_MODEL), _BF16)],
        compiler_params=pltpu.CompilerParams(
            dimension_semantics=("arbitrary",), vmem_limit_bytes=VMEM_LIMIT_BYTES),
        name="mix_out",
    )(xf, oa, ga, hc, hc, hc, cb, gc, osg, gs, cw_t, bw, w_out, fw)


def kernel(x, norm_w, w_in, q_norm_w, k_norm_w, conv_w, sgu_norm_w, sgu_w, sgu_b, branch_norm_w, w_out,
           final_norm_w):
    b, s, d = x.shape
    depth = w_in.shape[0]
    n = b * s
    rope_row, rope_col = _rope_tables(s)
    sw = sgu_w.astype(_BF16)
    xf = x.reshape(n, d)
    for l in range(depth):
        q, k, v, vt, ga, h = _attn_proj(xf, norm_w[l].reshape(1, d), w_in, l, q_norm_w[l].reshape(1, HEAD_DIM),
                                        k_norm_w[l].reshape(1, HEAD_DIM), rope_row, rope_col, s)
        hc, cb, gc, osg, gs = _rest_proj(h, w_in, l, sgu_norm_w[l].reshape(1, SGU_W), sw[l], sgu_b[l].T)
        oa = _attention(q, k, v, vt, q_norm_w[l], k_norm_w[l], b, s)
        xf = _mix(xf, oa, ga, hc, cb, gc, osg, gs, conv_w[l].T, branch_norm_w[l].reshape(1, MIX_W),
                  w_out, l, final_norm_w.reshape(1, d), s, final_norm=(l == depth - 1))
    return xf.reshape(b, s, d)
```
